```python
import jax, jax.numpy as jnp
from jax import lax
import numpy as np

D_MODEL = 1024
BATCH = 4
SEQ = 8192
DEPTH = 4

A_HEADS = 16
A_KV_HEADS = 4
A_HEAD_DIM = 64
A_REP = A_HEADS // A_KV_HEADS
IDX_HEADS = 8
IDX_DIM = 64
IDX_TOPK_MAX = 256
Q_BLOCK = 128
A_Q = A_HEADS * A_HEAD_DIM
A_KV = A_KV_HEADS * A_HEAD_DIM
A_IN = A_Q + 2 * A_KV + IDX_HEADS * IDX_DIM + IDX_DIM + IDX_HEADS
A_SPLITS = (A_Q, A_Q + A_KV, A_Q + 2 * A_KV, A_Q + 2 * A_KV + IDX_HEADS * IDX_DIM,
            A_Q + 2 * A_KV + IDX_HEADS * IDX_DIM + IDX_DIM)

B_HEADS = 4
B_DK = D_MODEL // 2 // B_HEADS
B_DV = D_MODEL // B_HEADS
B_GATE_RANK = 16
B_GATE_TAU = 16.0
B_CHUNK = 64
B_QK = B_HEADS * B_DK
B_V = B_HEADS * B_DV
B_IN = 2 * B_QK + 2 * B_V + B_GATE_RANK
B_SPLITS = (B_QK, 2 * B_QK, 2 * B_QK + B_V, 2 * B_QK + 2 * B_V)

D_FF = 4 * D_MODEL

ROPE_THETA = 500000.0
ROT_DIM = A_HEAD_DIM // 4
LN_EPS = 1e-5
RMS_EPS = 1e-6
DN_ALPHA = (2 * DEPTH) ** 0.25
DN_BETA = (8 * DEPTH) ** -0.25
N_A_LAYERS = (DEPTH + 1) // 2
N_B_LAYERS = DEPTH // 2

kernel_name = 'hybrid_dsa_gla_deepnorm'


def layer_norm(x, g, b):
    xf = x.astype(jnp.float32)
    mu = jnp.mean(xf, axis=-1, keepdims=True)
    var = jnp.mean(jnp.square(xf - mu), axis=-1, keepdims=True)
    return ((xf - mu) * lax.rsqrt(var + LN_EPS) * g + b).astype(x.dtype)


def rotary_tables(positions):
    inv = ROPE_THETA ** (-jnp.arange(0, ROT_DIM, 2, dtype=jnp.float32) / ROT_DIM)
    ang = positions.astype(jnp.float32)[..., None] * inv
    return jnp.cos(ang), jnp.sin(ang)


def partial_rotary(t, cos, sin):
    half = cos.shape[-1]
    shape = cos.shape[:2] + (1,) * (t.ndim - 3) + (half,)
    c = cos.reshape(shape)
    s = sin.reshape(shape)
    t1 = t[..., :half].astype(jnp.float32)
    t2 = t[..., half:2 * half].astype(jnp.float32)
    return jnp.concatenate([(t1 * c - t2 * s).astype(t.dtype),
                            (t2 * c + t1 * s).astype(t.dtype),
                            t[..., 2 * half:]], axis=-1)


def dsa_mixer(x, cos, sin, w_in, w_o):
    bsz, L, _ = x.shape
    top_k = min(IDX_TOPK_MAX, L // 4)
    proj = x @ w_in
    q, k, v, iq, ik, iw = jnp.split(proj, A_SPLITS, axis=-1)
    q = partial_rotary(q.reshape(bsz, L, A_HEADS, A_HEAD_DIM), cos, sin)
    k = partial_rotary(k.reshape(bsz, L, A_KV_HEADS, A_HEAD_DIM), cos, sin)
    v = v.reshape(bsz, L, A_KV_HEADS, A_HEAD_DIM)
    iq = partial_rotary(iq.reshape(bsz, L, IDX_HEADS, IDX_DIM), cos, sin)
    ik = partial_rotary(ik, cos, sin)
    iw = iw.astype(jnp.float32) * IDX_HEADS ** -0.5
    key_idx = jnp.arange(L)

    def block(i):
        start = i * Q_BLOCK
        q_idx = start + jnp.arange(Q_BLOCK)
        iq_b = lax.dynamic_slice_in_dim(iq, start, Q_BLOCK, axis=1)
        iw_b = lax.dynamic_slice_in_dim(iw, start, Q_BLOCK, axis=1)
        s = jnp.einsum('bqhd,bsd->bqhs', iq_b, ik, preferred_element_type=jnp.float32) * IDX_DIM ** -0.5
        score = jnp.einsum('bqh,bqhs->bqs', iw_b, jax.nn.relu(s))
        causal = key_idx[None, :] <= q_idx[:, None]
        score = jnp.where(causal[None], score, -jnp.inf)
        _, sel = lax.top_k(score, top_k)
        valid = sel <= q_idx[None, :, None]
        k_sel = jax.vmap(lambda kk, ii: kk[ii])(k, sel)
        v_sel = jax.vmap(lambda vv, ii: vv[ii])(v, sel)
        q_b = lax.dynamic_slice_in_dim(q, start, Q_BLOCK, axis=1).reshape(bsz, Q_BLOCK, A_KV_HEADS, A_REP, A_HEAD_DIM)
        logits = jnp.einsum('bqgrd,bqkgd->bqgrk', q_b, k_sel, preferred_element_type=jnp.float32) * A_HEAD_DIM ** -0.5
        logits = jnp.where(valid[:, :, None, None, :], logits, -jnp.inf)
        p = jax.nn.softmax(logits, axis=-1)
        o = jnp.einsum('bqgrk,bqkgd->bqgrd', p.astype(v.dtype), v_sel)
        return o.reshape(bsz, Q_BLOCK, A_Q)

    out = lax.map(block, jnp.arange(L // Q_BLOCK))
    out = out.transpose(1, 0, 2, 3).reshape(bsz, L, A_Q)
    return out @ w_o


def gla_mixer(x, w_in, w_a2, b_a, g_norm, w_o):
    bsz, L, _ = x.shape
    n_c = L // B_CHUNK
    proj = x @ w_in
    q, k, v, r, a_low = jnp.split(proj, B_SPLITS, axis=-1)
    log_a = jax.nn.log_sigmoid((a_low @ w_a2 + b_a).astype(jnp.float32)) / B_GATE_TAU

    def to_chunks(t, d):
        return t.astype(jnp.float32).reshape(bsz, n_c, B_CHUNK, B_HEADS, d).transpose(1, 0, 3, 2, 4)

    qc = to_chunks(q, B_DK) * B_DK ** -0.5
    kc = to_chunks(k, B_DK)
    vc = to_chunks(v, B_DV)
    gc = to_chunks(log_a, B_DK)
    mask = jnp.tril(jnp.ones((B_CHUNK, B_CHUNK), dtype=bool))

    def step(state, inp):
        qi, ki, vi, gi = inp
        b = jnp.cumsum(gi, axis=2)
        b_last = b[:, :, -1:, :]
        diff = jnp.where(mask[None, None, :, :, None], b[:, :, :, None, :] - b[:, :, None, :, :], -jnp.inf)
        attn = jnp.einsum('bhid,bhjd,bhijd->bhij', qi, ki, jnp.exp(diff))
        intra = jnp.einsum('bhij,bhje->bhie', attn, vi)
        inter = jnp.einsum('bhid,bhde->bhie', qi * jnp.exp(b), state)
        new_state = jnp.exp(b_last)[:, :, 0, :, None] * state + jnp.einsum('bhjd,bhje->bhde', ki * jnp.exp(b_last - b), vi)
        return new_state, intra + inter

    state0 = jnp.zeros((bsz, B_HEADS, B_DK, B_DV), jnp.float32)
    _, outs = lax.scan(step, state0, (qc, kc, vc, gc))
    o = outs.transpose(1, 0, 3, 2, 4).reshape(bsz, L, B_HEADS, B_DV)
    o = o * lax.rsqrt(jnp.mean(jnp.square(o), axis=-1, keepdims=True) + RMS_EPS) * g_norm
    gate = jax.nn.silu(r.astype(jnp.float32).reshape(bsz, L, B_HEADS, B_DV))
    y = (o * gate).reshape(bsz, L, B_V).astype(x.dtype)
    return y @ w_o


def squared_relu_mlp(x, w_up, w_down):
    return jnp.square(jax.nn.relu(x @ w_up)) @ w_down


def setup_inputs(seed: int = 0) -> dict:
    key = jax.random.key(seed)
    ks = jax.random.split(key, 16)
    f32 = jnp.float32

    def nrm(k, shape, fan_in, scale=1.0):
        return jax.random.normal(k, shape, f32) * (scale * fan_in ** -0.5)

    x = jax.random.normal(ks[0], (BATCH, SEQ, D_MODEL), f32)
    positions = jnp.broadcast_to(jnp.arange(SEQ, dtype=jnp.int32), (BATCH, SEQ))
    a_w_in = nrm(ks[1], (N_A_LAYERS, D_MODEL, A_IN), D_MODEL)
    a_w_o = nrm(ks[2], (N_A_LAYERS, A_Q, D_MODEL), A_Q, DN_BETA)
    b_w_in = nrm(ks[3], (N_B_LAYERS, D_MODEL, B_IN), D_MODEL)
    b_w_a2 = nrm(ks[4], (N_B_LAYERS, B_GATE_RANK, B_QK), B_GATE_RANK)
    b_b_a = 0.1 * jax.random.normal(ks[5], (N_B_LAYERS, B_QK), f32)
    b_g_norm = 1.0 + 0.02 * jax.random.normal(ks[6], (N_B_LAYERS, B_DV), f32)
    b_w_o = nrm(ks[7], (N_B_LAYERS, B_V, D_MODEL), B_V, DN_BETA)
    ln_mix_g = 1.0 + 0.02 * jax.random.normal(ks[8], (DEPTH, D_MODEL), f32)
    ln_mix_b = 0.02 * jax.random.normal(ks[9], (DEPTH, D_MODEL), f32)
    mlp_w_up = nrm(ks[10], (DEPTH, D_MODEL, D_FF), D_MODEL)
    mlp_w_down = nrm(ks[11], (DEPTH, D_FF, D_MODEL), D_FF, DN_BETA)
    ln_mlp_g = 1.0 + 0.02 * jax.random.normal(ks[12], (DEPTH, D_MODEL), f32)
    ln_mlp_b = 0.02 * jax.random.normal(ks[13], (DEPTH, D_MODEL), f32)
    return {'x': x, 'positions': positions, 'a_w_in': a_w_in, 'a_w_o': a_w_o,
            'b_w_in': b_w_in, 'b_w_a2': b_w_a2, 'b_b_a': b_b_a, 'b_g_norm': b_g_norm, 'b_w_o': b_w_o,
            'ln_mix_g': ln_mix_g, 'ln_mix_b': ln_mix_b, 'mlp_w_up': mlp_w_up, 'mlp_w_down': mlp_w_down,
            'ln_mlp_g': ln_mlp_g, 'ln_mlp_b': ln_mlp_b}


def reference(x, positions, a_w_in, a_w_o, b_w_in, b_w_a2, b_b_a, b_g_norm, b_w_o,
              ln_mix_g, ln_mix_b, mlp_w_up, mlp_w_down, ln_mlp_g, ln_mlp_b):
    cos, sin = rotary_tables(positions)
    h = x
    for i in range(DEPTH):
        j = i // 2
        if i % 2 == 0:
            mix = dsa_mixer(h, cos, sin, a_w_in[j], a_w_o[j])
        else:
            mix = gla_mixer(h, b_w_in[j], b_w_a2[j], b_b_a[j], b_g_norm[j], b_w_o[j])
        h = layer_norm(DN_ALPHA * h + mix, ln_mix_g[i], ln_mix_b[i])
        h = layer_norm(DN_ALPHA * h + squared_relu_mlp(h, mlp_w_up[i], mlp_w_down[i]), ln_mlp_g[i], ln_mlp_b[i])
    return h
```

```python
import functools

import jax
import jax.numpy as jnp
from jax import lax
from jax.experimental import pallas as pl
from jax.experimental.pallas import tpu as pltpu

D_MODEL = 1024
DEPTH = 4

A_HEADS = 16
A_KV_HEADS = 4
A_HEAD_DIM = 64
A_REP = A_HEADS // A_KV_HEADS
IDX_HEADS = 8
IDX_DIM = 64
IDX_TOPK_MAX = 256
A_Q = A_HEADS * A_HEAD_DIM
A_KV = A_KV_HEADS * A_HEAD_DIM
A_IQ = IDX_HEADS * IDX_DIM

B_HEADS = 4
B_DK = D_MODEL // 2 // B_HEADS
B_DV = D_MODEL // B_HEADS
B_GATE_RANK = 16
B_GATE_TAU = 16.0
B_CHUNK = 64
B_QK = B_HEADS * B_DK
B_V = B_HEADS * B_DV

D_FF = 4 * D_MODEL

ROPE_THETA = 500000.0
ROT_DIM = A_HEAD_DIM // 4
ROT_HALF = ROT_DIM // 2
LN_EPS = 1e-5
RMS_EPS = 1e-6
DN_ALPHA = (2 * DEPTH) ** 0.25

LANES = 128
VMEM_LIMIT_BYTES = 56 * 1024 * 1024
INT32_MIN = -(2 ** 31)
NEG_BIG = -1e30
EXP_CLAMP = 80.0

BF16 = jnp.bfloat16
F32 = jnp.float32

_NT = (((1,), (1,)), ((), ()))
_TN = (((0,), (0,)), ((), ()))


def _params(*sem):
    return pltpu.CompilerParams(dimension_semantics=sem,
                                vmem_limit_bytes=VMEM_LIMIT_BYTES)


def _const_spec(shape):
    nd = len(shape)
    return pl.BlockSpec(shape, lambda *_: (0,) * nd)


def _layer_norm(z, g, b):
    mu = jnp.mean(z, axis=-1, keepdims=True)
    zc = z - mu
    var = jnp.mean(zc * zc, axis=-1, keepdims=True)
    return zc * lax.rsqrt(var + LN_EPS) * g + b


def _rot_tables(pos, inv):
    ang = pos * inv
    c = jnp.cos(ang)
    s = jnp.sin(ang)
    lane = lax.broadcasted_iota(jnp.int32, ang.shape, 1) % A_HEAD_DIM
    s_lo = jnp.where(lane < ROT_HALF, -s, 0.0)
    s_hi = jnp.where((lane >= ROT_HALF) & (lane < ROT_DIM), s, 0.0)
    return c, s_lo, s_hi


def _rotate(t, c, s_lo, s_hi):
    w = t.shape[1]
    reps = w // LANES
    if reps > 1:
        c = jnp.concatenate([c] * reps, axis=1)
        s_lo = jnp.concatenate([s_lo] * reps, axis=1)
        s_hi = jnp.concatenate([s_hi] * reps, axis=1)
    up = pltpu.roll(t, w - ROT_HALF, axis=1)
    dn = pltpu.roll(t, ROT_HALF, axis=1)
    return t * c + up * s_lo + dn * s_hi


def _dsa_proj_kernel(h_ref, pos_ref, inv_ref, wq_ref, wk_ref, wv_ref, wiq_ref, wx_ref,
                     q_ref, k_ref, v_ref, iq_ref, ik_ref, iw_ref):
    xb = h_ref[...].astype(BF16)
    c, s_lo, s_hi = _rot_tables(pos_ref[...], inv_ref[...])

    q = jnp.dot(xb, wq_ref[...], preferred_element_type=F32)
    q = _rotate(q, c, s_lo, s_hi) * (A_HEAD_DIM ** -0.5)
    for hh in range(A_HEADS):
        q_ref[hh] = q[:, hh * A_HEAD_DIM:(hh + 1) * A_HEAD_DIM].astype(BF16)

    k = jnp.dot(xb, wk_ref[...], preferred_element_type=F32)
    k = _rotate(k, c, s_lo, s_hi)
    v = jnp.dot(xb, wv_ref[...], preferred_element_type=F32)
    for g in range(A_KV_HEADS):
        k_ref[g] = k[:, g * A_HEAD_DIM:(g + 1) * A_HEAD_DIM].astype(BF16)
        v_ref[g] = v[:, g * A_HEAD_DIM:(g + 1) * A_HEAD_DIM].astype(BF16)

    iq = jnp.dot(xb, wiq_ref[...], preferred_element_type=F32)
    iq = _rotate(iq, c, s_lo, s_hi) * (IDX_DIM ** -0.5)
    for hh in range(IDX_HEADS):
        iq_ref[hh] = iq[:, hh * IDX_DIM:(hh + 1) * IDX_DIM].astype(BF16)

    x = jnp.dot(xb, wx_ref[...], preferred_element_type=F32)
    first = lax.broadcasted_iota(jnp.int32, x.shape, 1) < IDX_DIM
    x = _rotate(x, jnp.where(first, c, 1.0), jnp.where(first, s_lo, 0.0), jnp.where(first, s_hi, 0.0))
    ik_ref[...] = x[:, :IDX_DIM].astype(BF16)
    iw_ref[...] = x[:, IDX_DIM:IDX_DIM + IDX_HEADS] * (IDX_HEADS ** -0.5)


def _dsa_proj(h, pos, inv, wq, wk, wv, wiq, wx, *, tm):
    t = h.shape[0]
    row = lambda i: (i, 0)
    head = lambda i: (0, i, 0)
    return pl.pallas_call(
        _dsa_proj_kernel,
        grid=(t // tm,),
        in_specs=[pl.BlockSpec((tm, D_MODEL), row), pl.BlockSpec((tm, 1), row),
                  _const_spec(inv.shape), _const_spec(wq.shape), _const_spec(wk.shape),
                  _const_spec(wv.shape), _const_spec(wiq.shape), _const_spec(wx.shape)],
        out_specs=[pl.BlockSpec((A_HEADS, tm, A_HEAD_DIM), head),
                   pl.BlockSpec((A_KV_HEADS, tm, A_HEAD_DIM), head),
                   pl.BlockSpec((A_KV_HEADS, tm, A_HEAD_DIM), head),
                   pl.BlockSpec((IDX_HEADS, tm, IDX_DIM), head),
                   pl.BlockSpec((tm, IDX_DIM), row),
                   pl.BlockSpec((tm, IDX_HEADS), row)],
        out_shape=[jax.ShapeDtypeStruct((A_HEADS, t, A_HEAD_DIM), BF16),
                   jax.ShapeDtypeStruct((A_KV_HEADS, t, A_HEAD_DIM), BF16),
                   jax.ShapeDtypeStruct((A_KV_HEADS, t, A_HEAD_DIM), BF16),
                   jax.ShapeDtypeStruct((IDX_HEADS, t, IDX_DIM), BF16),
                   jax.ShapeDtypeStruct((t, IDX_DIM), BF16),
                   jax.ShapeDtypeStruct((t, IDX_HEADS), F32)],
        compiler_params=_params("parallel"),
        name="dsa_proj",
    )(h, pos, inv, wq, wk, wv, wiq, wx)


def _dsa_attn_kernel(q_ref, iq_ref, iw_ref, k_ref, v_ref, ik_ref, o_ref,
                     slab, m_sc, l_sc, acc_sc, cut_sc, *, tq, top_k):
    i = pl.program_id(1)
    n_kt = i + 1
    q0 = i * tq
    row_pos = q0 + lax.broadcasted_iota(jnp.int32, (tq, 1), 0)
    lane_t = lax.broadcasted_iota(jnp.int32, (tq, tq), 1)
    lane_c = lax.broadcasted_iota(jnp.int32, (tq, LANES), 1)

    iw = iw_ref[...]

    def score_tile(kt, carry):
        ik_t = ik_ref[pl.ds(pl.multiple_of(kt * tq, tq), tq), :]
        sc = jnp.zeros((tq, tq), F32)
        for hh in range(IDX_HEADS):
            s = lax.dot_general(iq_ref[hh], ik_t, _NT, preferred_element_type=F32)
            sc = sc + iw[:, hh:hh + 1] * jnp.maximum(s, 0.0)
        bits = lax.bitcast_convert_type(sc, jnp.int32)
        key = bits ^ ((bits >> 31) & 0x7FFFFFFF)
        causal = (kt * tq + lane_t) <= row_pos
        slab[kt] = jnp.where(causal, key, INT32_MIN)
        return carry

    lax.fori_loop(0, n_kt, score_tile, 0)

    k_row = jnp.minimum(top_k, row_pos + 1)

    def count(pred):
        def body(kt, acc):
            keys = slab[kt]
            for c in range(tq // LANES):
                idx = kt * tq + c * LANES + lane_c
                acc = acc + jnp.where(pred(keys[:, c * LANES:(c + 1) * LANES], idx), 1, 0)
            return acc
        acc = lax.fori_loop(0, n_kt, body, jnp.zeros((tq, LANES), jnp.int32))
        return jnp.sum(acc, axis=1, keepdims=True)

    cnt0 = count(lambda ky, idx: ky >= 0)
    nonneg = cnt0 >= k_row
    thr0 = jnp.where(nonneg, 0, INT32_MIN)
    nge0 = jnp.where(nonneg, cnt0, n_kt * tq)

    def bit_round(r, carry):
        thr, nge = carry
        cand = thr | lax.shift_left(jnp.int32(1), 30 - r)
        cnt = count(lambda ky, idx: ky >= cand)
        ok = cnt >= k_row
        return jnp.where(ok, cand, thr), jnp.where(ok, cnt, nge)

    thr, nge = lax.fori_loop(0, 31, bit_round, (thr0, nge0))

    cut_sc[...] = jnp.full((tq, 1), n_kt * tq, jnp.int32)
    has_ties = jnp.max(nge - k_row) > 0

    @pl.when(has_ties)
    def _():
        n_gt = count(lambda ky, idx: ky > thr)
        need = k_row - n_gt

        def idx_round(r, cut):
            cand = cut | lax.shift_left(jnp.int32(1), 30 - r)
            before = count(lambda ky, idx: (ky == thr) & (idx < cand))
            return jnp.where(before < need, cand, cut)

        cut_sc[...] = lax.fori_loop(0, 31, idx_round, jnp.zeros((tq, 1), jnp.int32))

    cut = cut_sc[...]

    m_sc[...] = jnp.full(m_sc.shape, NEG_BIG, F32)
    l_sc[...] = jnp.zeros(l_sc.shape, F32)
    acc_sc[...] = jnp.zeros(acc_sc.shape, F32)

    def attn_tile(kt, carry):
        keys = slab[kt]
        idx = kt * tq + lane_t
        sel = (keys > thr) | ((keys == thr) & (idx <= cut))
        bias = jnp.where(sel, 0.0, NEG_BIG)
        start = pl.multiple_of(kt * tq, tq)
        for g in range(A_KV_HEADS):
            qg = q_ref[g * A_REP:(g + 1) * A_REP].reshape(A_REP * tq, A_HEAD_DIM)
            kg = k_ref[g, pl.ds(start, tq), :]
            vg = v_ref[g, pl.ds(start, tq), :]
            s = lax.dot_general(qg, kg, _NT, preferred_element_type=F32)
            s = (s.reshape(A_REP, tq, tq) + bias[None]).reshape(A_REP * tq, tq)
            m_prev = m_sc[g]
            m_new = jnp.maximum(m_prev, jnp.max(s, axis=1, keepdims=True))
            alpha = jnp.exp(m_prev - m_new)
            p = jnp.exp(s - m_new)
            l_sc[g] = alpha * l_sc[g] + jnp.sum(p, axis=1, keepdims=True)
            acc_sc[g] = alpha * acc_sc[g] + jnp.dot(p.astype(BF16), vg, preferred_element_type=F32)
            m_sc[g] = m_new
        return carry

    lax.fori_loop(0, n_kt, attn_tile, 0)

    for g in range(A_KV_HEADS):
        og = acc_sc[g] / l_sc[g]
        for r in range(A_REP):
            hh = g * A_REP + r
            o_ref[:, hh * A_HEAD_DIM:(hh + 1) * A_HEAD_DIM] = og[r * tq:(r + 1) * tq].astype(BF16)


def _dsa_attn(q, k, v, iq, ik, iw, *, batch, seq, tq):
    t = batch * seq
    nq = seq // tq
    top_k = min(IDX_TOPK_MAX, seq // 4)
    qtile = lambda b, i: (0, b * nq + i, 0)
    whole = lambda b, i: (0, b, 0)
    once = pl.Buffered(1)
    return pl.pallas_call(
        functools.partial(_dsa_attn_kernel, tq=tq, top_k=top_k),
        grid=(batch, nq),
        in_specs=[pl.BlockSpec((A_HEADS, tq, A_HEAD_DIM), qtile),
                  pl.BlockSpec((IDX_HEADS, tq, IDX_DIM), qtile),
                  pl.BlockSpec((tq, IDX_HEADS), lambda b, i: (b * nq + i, 0)),
                  pl.BlockSpec((A_KV_HEADS, seq, A_HEAD_DIM), whole, pipeline_mode=once),
                  pl.BlockSpec((A_KV_HEADS, seq, A_HEAD_DIM), whole, pipeline_mode=once),
                  pl.BlockSpec((seq, IDX_DIM), lambda b, i: (b, 0), pipeline_mode=once)],
        out_specs=pl.BlockSpec((tq, A_Q), lambda b, i: (b * nq + i, 0)),
        out_shape=jax.ShapeDtypeStruct((t, A_Q), BF16),
        scratch_shapes=[pltpu.VMEM((nq, tq, tq), jnp.int32),
                        pltpu.VMEM((A_KV_HEADS, A_REP * tq, 1), F32),
                        pltpu.VMEM((A_KV_HEADS, A_REP * tq, 1), F32),
                        pltpu.VMEM((A_KV_HEADS, A_REP * tq, A_HEAD_DIM), F32),
                        pltpu.VMEM((tq, 1), jnp.int32)],
        compiler_params=_params("parallel", "arbitrary"),
        name="dsa_attn",
    )(q, iq, iw, k, v, ik)


def _out_ln_kernel(y_ref, w_ref, h_ref, g_ref, b_ref, o_ref):
    mix = jnp.dot(y_ref[...], w_ref[...], preferred_element_type=F32)
    o_ref[...] = _layer_norm(DN_ALPHA * h_ref[...] + mix, g_ref[...], b_ref[...])


def _out_ln(y, w, h, g, b, *, tm):
    t, kdim = y.shape
    row = lambda i: (i, 0)
    return pl.pallas_call(
        _out_ln_kernel,
        grid=(t // tm,),
        in_specs=[pl.BlockSpec((tm, kdim), row), _const_spec(w.shape),
                  pl.BlockSpec((tm, D_MODEL), row), _const_spec(g.shape), _const_spec(b.shape)],
        out_specs=pl.BlockSpec((tm, D_MODEL), row),
        out_shape=jax.ShapeDtypeStruct((t, D_MODEL), F32),
        compiler_params=_params("parallel"),
        name="out_ln",
    )(y, w, h, g, b)


def _mlp_kernel(h_ref, wu_ref, wd_ref, g_ref, b_ref, o_ref, *, tf):
    h = h_ref[...]
    xb = h.astype(BF16)
    acc = jnp.zeros(h.shape, F32)
    for c in range(D_FF // tf):
        u = jnp.dot(xb, wu_ref[:, c * tf:(c + 1) * tf], preferred_element_type=F32)
        u = jnp.square(jnp.maximum(u, 0.0)).astype(BF16)
        acc = acc + jnp.dot(u, wd_ref[c * tf:(c + 1) * tf, :], preferred_element_type=F32)
    o_ref[...] = _layer_norm(DN_ALPHA * h + acc, g_ref[...], b_ref[...])


def _mlp(h, wu, wd, g, b, *, tm, tf):
    t = h.shape[0]
    row = lambda i: (i, 0)
    once = pl.Buffered(1)
    return pl.pallas_call(
        functools.partial(_mlp_kernel, tf=tf),
        grid=(t // tm,),
        in_specs=[pl.BlockSpec((tm, D_MODEL), row),
                  pl.BlockSpec(wu.shape, lambda i: (0, 0), pipeline_mode=once),
                  pl.BlockSpec(wd.shape, lambda i: (0, 0), pipeline_mode=once),
                  _const_spec(g.shape), _const_spec(b.shape)],
        out_specs=pl.BlockSpec((tm, D_MODEL), row),
        out_shape=jax.ShapeDtypeStruct((t, D_MODEL), F32),
        compiler_params=_params("parallel"),
        name="mlp_ln",
    )(h, wu, wd, g, b)


def _gla_proj_kernel(h_ref, wq_ref, wk_ref, wv_ref, wr_ref, wa_ref,
                     q_ref, k_ref, v_ref, r_ref, a_ref):
    xb = h_ref[...].astype(BF16)
    q = jnp.dot(xb, wq_ref[...], preferred_element_type=F32)
    q_ref[...] = (q * (B_DK ** -0.5)).astype(BF16)
    k_ref[...] = jnp.dot(xb, wk_ref[...], preferred_element_type=F32).astype(BF16)
    v_ref[...] = jnp.dot(xb, wv_ref[...], preferred_element_type=F32).astype(BF16)
    r_ref[...] = jnp.dot(xb, wr_ref[...], preferred_element_type=F32).astype(BF16)
    a = jnp.dot(xb, wa_ref[...], preferred_element_type=F32)
    a_ref[...] = a[:, :B_GATE_RANK]


def _gla_proj(h, wq, wk, wv, wr, wa, *, tm):
    t = h.shape[0]
    row = lambda i: (i, 0)
    return pl.pallas_call(
        _gla_proj_kernel,
        grid=(t // tm,),
        in_specs=[pl.BlockSpec((tm, D_MODEL), row), _const_spec(wq.shape), _const_spec(wk.shape),
                  _const_spec(wv.shape), _const_spec(wr.shape), _const_spec(wa.shape)],
        out_specs=[pl.BlockSpec((tm, B_QK), row), pl.BlockSpec((tm, B_QK), row),
                   pl.BlockSpec((tm, B_V), row), pl.BlockSpec((tm, B_V), row),
                   pl.BlockSpec((tm, B_GATE_RANK), row)],
        out_shape=[jax.ShapeDtypeStruct((t, B_QK), BF16), jax.ShapeDtypeStruct((t, B_QK), BF16),
                   jax.ShapeDtypeStruct((t, B_V), BF16), jax.ShapeDtypeStruct((t, B_V), BF16),
                   jax.ShapeDtypeStruct((t, B_GATE_RANK), F32)],
        compiler_params=_params("parallel"),
        name="gla_proj",
    )(h, wq, wk, wv, wr, wa)


def _gla_kernel(q_ref, k_ref, v_ref, r_ref, a_ref, w2_ref, ba_ref, gn_ref, y_ref, st_ref, *, rows):
    n_chunks = rows // B_CHUNK

    @pl.when(pl.program_id(2) == 0)
    def _():
        st_ref[...] = jnp.zeros(st_ref.shape, F32)

    x = jnp.dot(a_ref[...], w2_ref[...], preferred_element_type=F32,
                precision=lax.Precision.HIGHEST) + ba_ref[...]
    log_a = (jnp.minimum(x, 0.0) - jnp.log1p(jnp.exp(-jnp.abs(x)))) / B_GATE_TAU

    row_in_chunk = lax.broadcasted_iota(jnp.int32, (rows, B_DK), 0) % B_CHUNK
    b = log_a
    step = 1
    while step < B_CHUNK:
        b = b + jnp.where(row_in_chunk >= step, pltpu.roll(b, step, axis=0), 0.0)
        step *= 2

    b3 = b.reshape(n_chunks, B_CHUNK, B_DK)
    b_last = b3[:, B_CHUNK - 1:B_CHUNK, :]
    b_mid = b3[:, B_CHUNK // 2 - 1:B_CHUNK // 2, :]
    q3 = q_ref[...].astype(F32).reshape(n_chunks, B_CHUNK, B_DK)
    k3 = k_ref[...].astype(F32).reshape(n_chunks, B_CHUNK, B_DK)
    q_in = (q3 * jnp.exp(b3)).astype(BF16)
    k_out = (k3 * jnp.exp(b_last - b3)).astype(BF16)
    q_mid = (q3 * jnp.exp(jnp.minimum(b3 - b_mid, EXP_CLAMP))).astype(BF16)
    k_mid = (k3 * jnp.exp(jnp.minimum(b_mid - b3, EXP_CLAMP))).astype(BF16)
    decay = jnp.exp(b_last)

    tril = (lax.broadcasted_iota(jnp.int32, (B_CHUNK, B_CHUNK), 1)
            <= lax.broadcasted_iota(jnp.int32, (B_CHUNK, B_CHUNK), 0))
    gn = gn_ref[...]
    state = st_ref[...]
    for n in range(n_chunks):
        lo, hi = n * B_CHUNK, (n + 1) * B_CHUNK
        vn = v_ref[lo:hi, :]
        attn = lax.dot_general(q_mid[n], k_mid[n], _NT, preferred_element_type=F32)
        attn = jnp.where(tril, attn, 0.0).astype(BF16)
        o = jnp.dot(attn, vn, preferred_element_type=F32)
        o = o + lax.dot_general(q_in[n], state.astype(BF16), _NT, preferred_element_type=F32)
        upd = lax.dot_general(vn, k_out[n], _TN, preferred_element_type=F32)
        state = decay[n] * state + upd
        o = o * lax.rsqrt(jnp.mean(o * o, axis=-1, keepdims=True) + RMS_EPS) * gn
        rn = r_ref[lo:hi, :].astype(F32)
        gate = rn / (1.0 + jnp.exp(-rn))
        y_ref[lo:hi, :] = (o * gate).astype(BF16)
    st_ref[...] = state


def _gla(q, k, v, r, a, w2, ba, gn, *, batch, seq, rows):
    t = batch * seq
    nb = seq // rows
    tile = lambda b, h, j: (b * nb + j, h)
    return pl.pallas_call(
        functools.partial(_gla_kernel, rows=rows),
        grid=(batch, B_HEADS, nb),
        in_specs=[pl.BlockSpec((rows, B_DK), tile), pl.BlockSpec((rows, B_DK), tile),
                  pl.BlockSpec((rows, B_DV), tile), pl.BlockSpec((rows, B_DV), tile),
                  pl.BlockSpec((rows, B_GATE_RANK), lambda b, h, j: (b * nb + j, 0)),
                  pl.BlockSpec((B_GATE_RANK, B_DK), lambda b, h, j: (0, h)),
                  pl.BlockSpec((1, B_DK), lambda b, h, j: (0, h)),
                  _const_spec(gn.shape)],
        out_specs=pl.BlockSpec((rows, B_DV), tile),
        out_shape=jax.ShapeDtypeStruct((t, B_V), BF16),
        scratch_shapes=[pltpu.VMEM((B_DV, B_DK), F32)],
        compiler_params=_params("parallel", "parallel", "arbitrary"),
        name="gla",
    )(q, k, v, r, a, w2, ba, gn)


def _tile(n, pref):
    return pref if n % pref == 0 else n


def _forward(x, positions, a_w_in, a_w_o, b_w_in, b_w_a2, b_b_a, b_g_norm, b_w_o,
             ln_mix_g, ln_mix_b, mlp_w_up, mlp_w_down, ln_mlp_g, ln_mlp_b):
    batch, seq, _ = x.shape
    t = batch * seq
    tm = _tile(t, 512)
    tq = _tile(seq, 256)
    rows = _tile(seq, 512)

    h = x.reshape(t, D_MODEL)
    pos = positions.reshape(t, 1).astype(F32)
    inv = ROPE_THETA ** (-jnp.arange(0, ROT_DIM, 2, dtype=F32) / ROT_DIM)
    inv = jnp.concatenate([inv, inv, jnp.zeros((A_HEAD_DIM - ROT_DIM,), F32)])
    inv = jnp.tile(inv, LANES // A_HEAD_DIM).reshape(1, LANES)
    row = lambda p: p.reshape(1, -1)

    for layer in range(DEPTH):
        j = layer // 2
        if layer % 2 == 0:
            w = a_w_in[j].astype(BF16)
            o1, o2, o3, o4 = A_Q, A_Q + A_KV, A_Q + 2 * A_KV, A_Q + 2 * A_KV + A_IQ
            wx = jnp.pad(w[:, o4:], ((0, 0), (0, LANES - (IDX_DIM + IDX_HEADS))))
            q, k, v, iq, ik, iw = _dsa_proj(h, pos, inv, w[:, :o1], w[:, o1:o2], w[:, o2:o3],
                                            w[:, o3:o4], wx, tm=tm)
            y = _dsa_attn(q, k, v, iq, ik, iw, batch=batch, seq=seq, tq=tq)
            w_o = a_w_o[j].astype(BF16)
        else:
            w = b_w_in[j].astype(BF16)
            o1, o2, o3, o4 = B_QK, 2 * B_QK, 2 * B_QK + B_V, 2 * B_QK + 2 * B_V
            wa = jnp.pad(w[:, o4:], ((0, 0), (0, LANES - B_GATE_RANK)))
            q, k, v, r, a = _gla_proj(h, w[:, :o1], w[:, o1:o2], w[:, o2:o3], w[:, o3:o4], wa, tm=tm)
            y = _gla(q, k, v, r, a, b_w_a2[j], row(b_b_a[j]), row(b_g_norm[j]),
                     batch=batch, seq=seq, rows=rows)
            w_o = b_w_o[j].astype(BF16)
        h = _out_ln(y, w_o, h, row(ln_mix_g[layer]), row(ln_mix_b[layer]), tm=tm)
        h = _mlp(h, mlp_w_up[layer].astype(BF16), mlp_w_down[layer].astype(BF16),
                 row(ln_mlp_g[layer]), row(ln_mlp_b[layer]), tm=tm, tf=1024)
    return h.reshape(batch, seq, D_MODEL)


_forward_jit = jax.jit(_forward)


def kernel(x, positions, a_w_in, a_w_o, b_w_in, b_w_a2, b_b_a, b_g_norm, b_w_o,
           ln_mix_g, ln_mix_b, mlp_w_up, mlp_w_down, ln_mlp_g, ln_mlp_b):
    return _forward_jit(x, positions, a_w_in, a_w_o, b_w_in, b_w_a2, b_b_a, b_g_norm, b_w_o,
                        ln_mix_g, ln_mix_b, mlp_w_up, mlp_w_down, ln_mlp_g, ln_mlp_b)
```

```python
import functools

import jax
import jax.numpy as jnp
from jax import lax
from jax.experimental import pallas as pl
from jax.experimental.pallas import tpu as pltpu

D_MODEL = 1024
DEPTH = 4

A_HEADS = 16
A_KV_HEADS = 4
A_HEAD_DIM = 64
A_REP = A_HEADS // A_KV_HEADS
IDX_HEADS = 8
IDX_DIM = 64
IDX_TOPK_MAX = 256
A_Q = A_HEADS * A_HEAD_DIM
A_KV = A_KV_HEADS * A_HEAD_DIM
A_IQ = IDX_HEADS * IDX_DIM

B_HEADS = 4
B_DK = D_MODEL // 2 // B_HEADS
B_DV = D_MODEL // B_HEADS
B_GATE_RANK = 16
B_GATE_TAU = 16.0
B_CHUNK = 64
B_QK = B_HEADS * B_DK
B_V = B_HEADS * B_DV

D_FF = 4 * D_MODEL

ROPE_THETA = 500000.0
ROT_DIM = A_HEAD_DIM // 4
ROT_HALF = ROT_DIM // 2
LN_EPS = 1e-5
RMS_EPS = 1e-6
DN_ALPHA = (2 * DEPTH) ** 0.25

LANES = 128
VMEM_LIMIT_BYTES = 56 * 1024 * 1024
INT32_MIN = -(2 ** 31)
INT32_MAX = 2 ** 31 - 1
NEG_BIG = -(2.0 ** 100)
EXP_CLAMP = 80.0
LOG2_E = 1.4426950408889634
PV_ROWS = A_HEAD_DIM + 16
SEARCH_VALUE_ROUNDS = 24

BF16 = jnp.bfloat16
F32 = jnp.float32

_NT = (((1,), (1,)), ((), ()))
_TN = (((0,), (0,)), ((), ()))


def _params(*sem):
    return pltpu.CompilerParams(dimension_semantics=sem,
                                vmem_limit_bytes=VMEM_LIMIT_BYTES)


def _const_spec(shape):
    nd = len(shape)
    return pl.BlockSpec(shape, lambda *_: (0,) * nd)


def _float_to_key(x):
    bits = lax.bitcast_convert_type(x, jnp.int32)
    return bits ^ ((bits >> 31) & INT32_MAX)


def _key_to_float(key):
    return lax.bitcast_convert_type(key ^ ((key >> 31) & INT32_MAX), F32)


def _layer_norm(z, g, b):
    mu = jnp.mean(z, axis=-1, keepdims=True)
    zc = z - mu
    var = jnp.mean(zc * zc, axis=-1, keepdims=True)
    return zc * lax.rsqrt(var + LN_EPS) * g + b


def _rot_tables(pos, inv):
    ang = pos * inv
    c = jnp.cos(ang)
    s = jnp.sin(ang)
    lane = lax.broadcasted_iota(jnp.int32, ang.shape, 1) % A_HEAD_DIM
    s_lo = jnp.where(lane < ROT_HALF, -s, 0.0)
    s_hi = jnp.where((lane >= ROT_HALF) & (lane < ROT_DIM), s, 0.0)
    return c, s_lo, s_hi


def _rotate(t, c, s_lo, s_hi):
    w = t.shape[1]
    reps = w // LANES
    if reps > 1:
        c = jnp.concatenate([c] * reps, axis=1)
        s_lo = jnp.concatenate([s_lo] * reps, axis=1)
        s_hi = jnp.concatenate([s_hi] * reps, axis=1)
    up = pltpu.roll(t, w - ROT_HALF, axis=1)
    dn = pltpu.roll(t, ROT_HALF, axis=1)
    return t * c + up * s_lo + dn * s_hi


def _dsa_proj_kernel(h_ref, pos_ref, inv_ref, wq_ref, wk_ref, wvt_ref, wiq_ref, wik_ref, wiwt_ref,
                     q_ref, k_ref, vt_ref, iq_ref, ik_ref, iwt_ref, *, tkv):
    xb = h_ref[...].astype(BF16)
    tm = xb.shape[0]
    c, s_lo, s_hi = _rot_tables(pos_ref[...], inv_ref[...])

    q = jnp.dot(xb, wq_ref[...], preferred_element_type=F32)
    q = _rotate(q, c, s_lo, s_hi) * (A_HEAD_DIM ** -0.5 * LOG2_E)
    for hh in range(A_HEADS):
        q_ref[hh] = q[:, hh * A_HEAD_DIM:(hh + 1) * A_HEAD_DIM].astype(BF16)

    k = jnp.dot(xb, wk_ref[...], preferred_element_type=F32)
    k = _rotate(k, c, s_lo, s_hi)
    vt = lax.dot_general(wvt_ref[...], xb, _NT, preferred_element_type=F32)
    ones_row = jnp.where(lax.broadcasted_iota(jnp.int32, (PV_ROWS - A_HEAD_DIM, tkv), 0) == 0, 1.0, 0.0)
    for g in range(A_KV_HEADS):
        k_ref[g] = k[:, g * A_HEAD_DIM:(g + 1) * A_HEAD_DIM].astype(BF16)
        for j in range(tm // tkv):
            vt_gj = vt[g * A_HEAD_DIM:(g + 1) * A_HEAD_DIM, j * tkv:(j + 1) * tkv]
            vt_ref[g, j] = jnp.concatenate([vt_gj, ones_row], axis=0).astype(BF16)

    iq = jnp.dot(xb, wiq_ref[...], preferred_element_type=F32)
    iq = _rotate(iq, c, s_lo, s_hi) * (IDX_DIM ** -0.5)
    for hh in range(IDX_HEADS):
        iq_ref[hh] = iq[:, hh * IDX_DIM:(hh + 1) * IDX_DIM].astype(BF16)

    x = jnp.dot(xb, wik_ref[...], preferred_element_type=F32)
    first = lax.broadcasted_iota(jnp.int32, x.shape, 1) < IDX_DIM
    x = _rotate(x, jnp.where(first, c, 1.0), jnp.where(first, s_lo, 0.0), jnp.where(first, s_hi, 0.0))
    ik_ref[...] = x[:, :IDX_DIM].astype(BF16)
    iwt = lax.dot_general(wiwt_ref[...], xb, _NT, preferred_element_type=F32)
    iwt_ref[...] = iwt[:IDX_HEADS] * (IDX_HEADS ** -0.5)


def _dsa_proj(h, pos, inv, wq, wk, wvt, wiq, wik, wiwt, *, tm, tkv):
    t = h.shape[0]
    row = lambda i: (i, 0)
    head = lambda i: (0, i, 0)
    return pl.pallas_call(
        functools.partial(_dsa_proj_kernel, tkv=tkv),
        grid=(t // tm,),
        in_specs=[pl.BlockSpec((tm, D_MODEL), row), pl.BlockSpec((tm, 1), row),
                  _const_spec(inv.shape), _const_spec(wq.shape), _const_spec(wk.shape),
                  _const_spec(wvt.shape), _const_spec(wiq.shape), _const_spec(wik.shape),
                  _const_spec(wiwt.shape)],
        out_specs=[pl.BlockSpec((A_HEADS, tm, A_HEAD_DIM), head),
                   pl.BlockSpec((A_KV_HEADS, tm, A_HEAD_DIM), head),
                   pl.BlockSpec((A_KV_HEADS, tm // tkv, PV_ROWS, tkv), lambda i: (0, i, 0, 0)),
                   pl.BlockSpec((IDX_HEADS, tm, IDX_DIM), head),
                   pl.BlockSpec((tm, IDX_DIM), row),
                   pl.BlockSpec((IDX_HEADS, tm), lambda i: (0, i))],
        out_shape=[jax.ShapeDtypeStruct((A_HEADS, t, A_HEAD_DIM), BF16),
                   jax.ShapeDtypeStruct((A_KV_HEADS, t, A_HEAD_DIM), BF16),
                   jax.ShapeDtypeStruct((A_KV_HEADS, t // tkv, PV_ROWS, tkv), BF16),
                   jax.ShapeDtypeStruct((IDX_HEADS, t, IDX_DIM), BF16),
                   jax.ShapeDtypeStruct((t, IDX_DIM), BF16),
                   jax.ShapeDtypeStruct((IDX_HEADS, t), F32)],
        compiler_params=_params("parallel"),
        name="dsa_proj",
    )(h, pos, inv, wq, wk, wvt, wiq, wik, wiwt)


def _dsa_attn_kernel(q_ref, iq_ref, iwt_ref, k_ref, vt_ref, ik_ref, o_ref,
                     slab, m_sc, acc_sc, cut_sc, ot_sc, *, tq, top_k):
    i = pl.program_id(1)
    n_kt = i + 1
    q_pos = i * tq + lax.broadcasted_iota(jnp.int32, (1, tq), 1)
    key_off = lax.broadcasted_iota(jnp.int32, (tq, tq), 0)

    iw = iwt_ref[...]

    def fold(x, op):
        return op(x.reshape(tq // 8, 8, tq), axis=0)

    def score_tile(kt, carry):
        mx, mn, c0, c1 = carry
        ik_t = ik_ref[pl.ds(pl.multiple_of(kt * tq, tq), tq), :]
        sc = jnp.zeros((tq, tq), F32)
        for hh in range(IDX_HEADS):
            s = lax.dot_general(ik_t, iq_ref[hh], _NT, preferred_element_type=F32)
            sc = sc + iw[hh:hh + 1, :] * jnp.maximum(s, 0.0)
        key = _float_to_key(sc)
        causal = (kt * tq + key_off) <= q_pos
        key = jnp.where(causal, key, INT32_MIN)
        slab[kt] = key
        mx = jnp.maximum(mx, fold(key, jnp.max))
        mn = jnp.minimum(mn, fold(jnp.where(causal, key, INT32_MAX), jnp.min))
        c0 = c0 + fold(jnp.where(key >= 0, 1, 0), jnp.sum)
        c1 = c1 + fold(jnp.where(key >= 1, 1, 0), jnp.sum)
        return mx, mn, c0, c1

    init = (jnp.full((8, tq), INT32_MIN, jnp.int32), jnp.full((8, tq), INT32_MAX, jnp.int32),
            jnp.zeros((8, tq), jnp.int32), jnp.zeros((8, tq), jnp.int32))
    mx, mn, c0, c1 = lax.fori_loop(0, n_kt, score_tile, init)
    key_max = jnp.max(mx, axis=0, keepdims=True)
    key_min = jnp.min(mn, axis=0, keepdims=True)
    cnt_nonneg = jnp.sum(c0, axis=0, keepdims=True)
    cnt_pos = jnp.sum(c1, axis=0, keepdims=True)

    k_row = jnp.minimum(top_k, q_pos + 1)

    def count(pred):
        def body(kt, acc):
            hit = jnp.where(pred(slab[kt], kt * tq + key_off), 1, 0)
            return acc + fold(hit, jnp.sum)
        acc = lax.fori_loop(0, n_kt, body, jnp.zeros((8, tq), jnp.int32))
        return jnp.sum(acc, axis=0, keepdims=True)

    pos_case = cnt_pos >= k_row
    zero_case = jnp.logical_not(pos_case) & (cnt_nonneg >= k_row)
    lo0 = jnp.where(pos_case, 1, jnp.where(zero_case, 0, key_min))
    clo0 = jnp.where(pos_case, cnt_pos, jnp.where(zero_case, cnt_nonneg, q_pos + 1))
    hi0 = jnp.where(pos_case, key_max + 1, jnp.where(zero_case, 1, 0))
    chi0 = jnp.where(pos_case, 0, jnp.where(zero_case, cnt_pos, cnt_nonneg))

    def n_active(lo, hi, clo):
        return jnp.max(jnp.where((clo != k_row) & (hi - 1 > lo), 1, 0))

    def search_cond(carry):
        it, nact = carry[0], carry[1]
        return (nact > 0) & (it < SEARCH_VALUE_ROUNDS + 32)

    def search_body(carry):
        it, _, lo, hi, clo, chi = carry
        active = (clo != k_row) & (hi - 1 > lo)
        lov = _key_to_float(lo)
        hiv = _key_to_float(hi)
        frac_interp = ((clo - k_row).astype(F32) - 0.5) / (clo - chi).astype(F32)
        bisect = (it < 3) | (it % 3 == 2)
        frac = jnp.where(bisect, 0.5, frac_interp)
        cand_value = _float_to_key(lov + (hiv - lov) * frac)
        cand_key = lo + ((hi - lo) >> 1)
        cand = jnp.where(it >= SEARCH_VALUE_ROUNDS, cand_key, cand_value)
        cand = jnp.minimum(jnp.maximum(cand, lo + 1), hi - 1)
        cand = jnp.where(active, cand, lo)
        cnt = count(lambda ky, idx: ky >= cand)
        ok = cnt >= k_row
        lo = jnp.where(ok, cand, lo)
        clo = jnp.where(ok, cnt, clo)
        hi = jnp.where(ok, hi, cand)
        chi = jnp.where(ok, chi, cnt)
        return it + 1, n_active(lo, hi, clo), lo, hi, clo, chi

    _, _, thr, _, nge, _ = lax.while_loop(
        search_cond, search_body, (jnp.int32(0), n_active(lo0, hi0, clo0), lo0, hi0, clo0, chi0))

    cut_sc[...] = jnp.full((1, tq), n_kt * tq, jnp.int32)
    has_ties = jnp.max(nge - k_row) > 0

    @pl.when(has_ties)
    def _():
        n_gt = count(lambda ky, idx: ky > thr)
        need = k_row - n_gt

        def idx_round(r, cut):
            cand = cut | lax.shift_left(jnp.int32(1), 30 - r)
            before = count(lambda ky, idx: (ky == thr) & (idx < cand))
            return jnp.where(before < need, cand, cut)

        cut_sc[...] = lax.fori_loop(0, 31, idx_round, jnp.zeros((1, tq), jnp.int32))

    cut = cut_sc[...]

    m_sc[...] = jnp.full(m_sc.shape, NEG_BIG, F32)
    acc_sc[...] = jnp.zeros(acc_sc.shape, F32)

    def mask_bias(kt):
        keys = slab[kt]
        sel = (keys > thr) | ((keys == thr) & ((kt * tq + key_off) <= cut))
        bias = jnp.where(sel, 0.0, NEG_BIG).astype(BF16)
        return jnp.concatenate([bias] * A_REP, axis=1)

    def logits(kt, g, bias):
        qg = q_ref[g * A_REP:(g + 1) * A_REP].reshape(A_REP * tq, A_HEAD_DIM)
        kg = k_ref[g, pl.ds(pl.multiple_of(kt * tq, tq), tq), :]
        return lax.dot_general(kg, qg, _NT, preferred_element_type=F32).astype(BF16) + bias

    def attn_tile(kt, carry):
        s_next, bias = carry
        for g in range(A_KV_HEADS):
            s = s_next
            if g + 1 < A_KV_HEADS:
                s_next = logits(kt, g + 1, bias)
            else:
                kt_next = jnp.minimum(kt + 1, n_kt - 1)
                bias = mask_bias(kt_next)
                s_next = logits(kt_next, 0, bias)
            m_prev = m_sc[g]
            m_new = jnp.maximum(m_prev, jnp.max(s, axis=0, keepdims=True).astype(F32))
            alpha = jnp.exp2(m_prev - m_new)
            p = jnp.exp2(s - m_new.astype(BF16))
            acc_sc[g] = alpha * acc_sc[g] + jnp.dot(vt_ref[g, kt], p, preferred_element_type=F32)
            m_sc[g] = m_new
        return s_next, bias

    bias0 = mask_bias(0)
    lax.fori_loop(0, n_kt, attn_tile, (logits(0, 0, bias0), bias0))

    for g in range(A_KV_HEADS):
        acc = acc_sc[g]
        og = acc[:A_HEAD_DIM] / acc[A_HEAD_DIM:A_HEAD_DIM + 1]
        for r in range(A_REP):
            hh = g * A_REP + r
            ot_sc[hh * A_HEAD_DIM:(hh + 1) * A_HEAD_DIM, :] = og[:, r * tq:(r + 1) * tq]
    o_ref[...] = ot_sc[...].T.astype(BF16)


def _dsa_attn(q, k, vt, iq, ik, iwt, *, batch, seq, tq):
    t = batch * seq
    nq = seq // tq
    top_k = min(IDX_TOPK_MAX, seq // 4)
    qtile = lambda b, i: (0, b * nq + i, 0)
    whole = lambda b, i: (0, b, 0)
    once = pl.Buffered(1)
    return pl.pallas_call(
        functools.partial(_dsa_attn_kernel, tq=tq, top_k=top_k),
        grid=(batch, nq),
        in_specs=[pl.BlockSpec((A_HEADS, tq, A_HEAD_DIM), qtile),
                  pl.BlockSpec((IDX_HEADS, tq, IDX_DIM), qtile),
                  pl.BlockSpec((IDX_HEADS, tq), lambda b, i: (0, b * nq + i)),
                  pl.BlockSpec((A_KV_HEADS, seq, A_HEAD_DIM), whole, pipeline_mode=once),
                  pl.BlockSpec((A_KV_HEADS, nq, PV_ROWS, tq), lambda b, i: (0, b, 0, 0),
                               pipeline_mode=once),
                  pl.BlockSpec((seq, IDX_DIM), lambda b, i: (b, 0), pipeline_mode=once)],
        out_specs=pl.BlockSpec((tq, A_Q), lambda b, i: (b * nq + i, 0)),
        out_shape=jax.ShapeDtypeStruct((t, A_Q), BF16),
        scratch_shapes=[pltpu.VMEM((nq, tq, tq), jnp.int32),
                        pltpu.VMEM((A_KV_HEADS, 1, A_REP * tq), F32),
                        pltpu.VMEM((A_KV_HEADS, PV_ROWS, A_REP * tq), F32),
                        pltpu.VMEM((1, tq), jnp.int32),
                        pltpu.VMEM((A_Q, tq), F32)],
        compiler_params=_params("parallel", "arbitrary"),
        name="dsa_attn",
    )(q, iq, iwt, k, vt, ik)


def _out_ln_kernel(y_ref, w_ref, h_ref, g_ref, b_ref, o_ref):
    mix = jnp.dot(y_ref[...], w_ref[...], preferred_element_type=F32)
    o_ref[...] = _layer_norm(DN_ALPHA * h_ref[...] + mix, g_ref[...], b_ref[...])


def _out_ln(y, w, h, g, b, *, tm):
    t, kdim = y.shape
    row = lambda i: (i, 0)
    return pl.pallas_call(
        _out_ln_kernel,
        grid=(t // tm,),
        in_specs=[pl.BlockSpec((tm, kdim), row), _const_spec(w.shape),
                  pl.BlockSpec((tm, D_MODEL), row), _const_spec(g.shape), _const_spec(b.shape)],
        out_specs=pl.BlockSpec((tm, D_MODEL), row),
        out_shape=jax.ShapeDtypeStruct((t, D_MODEL), F32),
        compiler_params=_params("parallel"),
        name="out_ln",
    )(y, w, h, g, b)


def _mlp_kernel(h_ref, wu_ref, wd_ref, g_ref, b_ref, o_ref, *, tf):
    h = h_ref[...]
    xb = h.astype(BF16)
    acc = jnp.zeros(h.shape, F32)
    for c in range(D_FF // tf):
        u = jnp.dot(xb, wu_ref[:, c * tf:(c + 1) * tf], preferred_element_type=F32)
        u = jnp.square(jnp.maximum(u, 0.0)).astype(BF16)
        acc = acc + jnp.dot(u, wd_ref[c * tf:(c + 1) * tf, :], preferred_element_type=F32)
    o_ref[...] = _layer_norm(DN_ALPHA * h + acc, g_ref[...], b_ref[...])


def _mlp(h, wu, wd, g, b, *, tm, tf):
    t = h.shape[0]
    row = lambda i: (i, 0)
    once = pl.Buffered(1)
    return pl.pallas_call(
        functools.partial(_mlp_kernel, tf=tf),
        grid=(t // tm,),
        in_specs=[pl.BlockSpec((tm, D_MODEL), row),
                  pl.BlockSpec(wu.shape, lambda i: (0, 0), pipeline_mode=once),
                  pl.BlockSpec(wd.shape, lambda i: (0, 0), pipeline_mode=once),
                  _const_spec(g.shape), _const_spec(b.shape)],
        out_specs=pl.BlockSpec((tm, D_MODEL), row),
        out_shape=jax.ShapeDtypeStruct((t, D_MODEL), F32),
        compiler_params=_params("parallel"),
        name="mlp_ln",
    )(h, wu, wd, g, b)


def _gla_proj_kernel(h_ref, wq_ref, wk_ref, wv_ref, wr_ref, wa_ref,
                     q_ref, k_ref, v_ref, r_ref, a_ref):
    xb = h_ref[...].astype(BF16)
    q = jnp.dot(xb, wq_ref[...], preferred_element_type=F32)
    q_ref[...] = (q * (B_DK ** -0.5)).astype(BF16)
    k_ref[...] = jnp.dot(xb, wk_ref[...], preferred_element_type=F32).astype(BF16)
    v_ref[...] = jnp.dot(xb, wv_ref[...], preferred_element_type=F32).astype(BF16)
    r_ref[...] = jnp.dot(xb, wr_ref[...], preferred_element_type=F32).astype(BF16)
    a = jnp.dot(xb, wa_ref[...], preferred_element_type=F32)
    a_ref[...] = a[:, :B_GATE_RANK]


def _gla_proj(h, wq, wk, wv, wr, wa, *, tm):
    t = h.shape[0]
    row = lambda i: (i, 0)
    return pl.pallas_call(
        _gla_proj_kernel,
        grid=(t // tm,),
        in_specs=[pl.BlockSpec((tm, D_MODEL), row), _const_spec(wq.shape), _const_spec(wk.shape),
                  _const_spec(wv.shape), _const_spec(wr.shape), _const_spec(wa.shape)],
        out_specs=[pl.BlockSpec((tm, B_QK), row), pl.BlockSpec((tm, B_QK), row),
                   pl.BlockSpec((tm, B_V), row), pl.BlockSpec((tm, B_V), row),
                   pl.BlockSpec((tm, B_GATE_RANK), row)],
        out_shape=[jax.ShapeDtypeStruct((t, B_QK), BF16), jax.ShapeDtypeStruct((t, B_QK), BF16),
                   jax.ShapeDtypeStruct((t, B_V), BF16), jax.ShapeDtypeStruct((t, B_V), BF16),
                   jax.ShapeDtypeStruct((t, B_GATE_RANK), F32)],
        compiler_params=_params("parallel"),
        name="gla_proj",
    )(h, wq, wk, wv, wr, wa)


def _gla_kernel(q_ref, k_ref, v_ref, r_ref, a_ref, w2_ref, ba_ref, gn_ref, y_ref, st_ref, *, rows):
    n_chunks = rows // B_CHUNK

    @pl.when(pl.program_id(2) == 0)
    def _():
        st_ref[...] = jnp.zeros(st_ref.shape, F32)

    x = jnp.dot(a_ref[...], w2_ref[...], preferred_element_type=F32,
                precision=lax.Precision.HIGHEST) + ba_ref[...]
    log_a = (jnp.minimum(x, 0.0) - jnp.log1p(jnp.exp(-jnp.abs(x)))) / B_GATE_TAU

    row_in_chunk = lax.broadcasted_iota(jnp.int32, (rows, B_DK), 0) % B_CHUNK
    b = log_a
    step = 1
    while step < B_CHUNK:
        b = b + jnp.where(row_in_chunk >= step, pltpu.roll(b, step, axis=0), 0.0)
        step *= 2

    b3 = b.reshape(n_chunks, B_CHUNK, B_DK)
    b_last = b3[:, B_CHUNK - 1:B_CHUNK, :]
    b_mid = b3[:, B_CHUNK // 2 - 1:B_CHUNK // 2, :]
    q3 = q_ref[...].astype(F32).reshape(n_chunks, B_CHUNK, B_DK)
    k3 = k_ref[...].astype(F32).reshape(n_chunks, B_CHUNK, B_DK)
    q_in = (q3 * jnp.exp(b3)).astype(BF16)
    k_out = (k3 * jnp.exp(b_last - b3)).astype(BF16)
    q_mid = (q3 * jnp.exp(jnp.minimum(b3 - b_mid, EXP_CLAMP))).astype(BF16)
    k_mid = (k3 * jnp.exp(jnp.minimum(b_mid - b3, EXP_CLAMP))).astype(BF16)
    decay = jnp.exp(b_last)

    tril = (lax.broadcasted_iota(jnp.int32, (B_CHUNK, B_CHUNK), 1)
            <= lax.broadcasted_iota(jnp.int32, (B_CHUNK, B_CHUNK), 0))
    gn = gn_ref[...]
    state = st_ref[...]
    for n in range(n_chunks):
        lo, hi = n * B_CHUNK, (n + 1) * B_CHUNK
        vn = v_ref[lo:hi, :]
        attn = lax.dot_general(q_mid[n], k_mid[n], _NT, preferred_element_type=F32)
        attn = jnp.where(tril, attn, 0.0).astype(BF16)
        o = jnp.dot(attn, vn, preferred_element_type=F32)
        o = o + lax.dot_general(q_in[n], state.astype(BF16), _NT, preferred_element_type=F32)
        upd = lax.dot_general(vn, k_out[n], _TN, preferred_element_type=F32)
        state = decay[n] * state + upd
        o = o * lax.rsqrt(jnp.mean(o * o, axis=-1, keepdims=True) + RMS_EPS) * gn
        rn = r_ref[lo:hi, :].astype(F32)
        gate = rn / (1.0 + jnp.exp(-rn))
        y_ref[lo:hi, :] = (o * gate).astype(BF16)
    st_ref[...] = state


def _gla(q, k, v, r, a, w2, ba, gn, *, batch, seq, rows):
    t = batch * seq
    nb = seq // rows
    tile = lambda b, h, j: (b * nb + j, h)
    return pl.pallas_call(
        functools.partial(_gla_kernel, rows=rows),
        grid=(batch, B_HEADS, nb),
        in_specs=[pl.BlockSpec((rows, B_DK), tile), pl.BlockSpec((rows, B_DK), tile),
                  pl.BlockSpec((rows, B_DV), tile), pl.BlockSpec((rows, B_DV), tile),
                  pl.BlockSpec((rows, B_GATE_RANK), lambda b, h, j: (b * nb + j, 0)),
                  pl.BlockSpec((B_GATE_RANK, B_DK), lambda b, h, j: (0, h)),
                  pl.BlockSpec((1, B_DK), lambda b, h, j: (0, h)),
                  _const_spec(gn.shape)],
        out_specs=pl.BlockSpec((rows, B_DV), tile),
        out_shape=jax.ShapeDtypeStruct((t, B_V), BF16),
        scratch_shapes=[pltpu.VMEM((B_DV, B_DK), F32)],
        compiler_params=_params("parallel", "parallel", "arbitrary"),
        name="gla",
    )(q, k, v, r, a, w2, ba, gn)


def _tile(n, pref):
    return pref if n % pref == 0 else n


def _forward(x, positions, a_w_in, a_w_o, b_w_in, b_w_a2, b_b_a, b_g_norm, b_w_o,
             ln_mix_g, ln_mix_b, mlp_w_up, mlp_w_down, ln_mlp_g, ln_mlp_b):
    batch, seq, _ = x.shape
    t = batch * seq
    tm = _tile(t, 512)
    tq = _tile(seq, 256)
    rows = _tile(seq, 512)

    h = x.reshape(t, D_MODEL)
    pos = positions.reshape(t, 1).astype(F32)
    inv = ROPE_THETA ** (-jnp.arange(0, ROT_DIM, 2, dtype=F32) / ROT_DIM)
    inv = jnp.concatenate([inv, inv, jnp.zeros((A_HEAD_DIM - ROT_DIM,), F32)])
    inv = jnp.tile(inv, LANES // A_HEAD_DIM).reshape(1, LANES)
    row = lambda p: p.reshape(1, -1)

    for layer in range(DEPTH):
        j = layer // 2
        if layer % 2 == 0:
            w = a_w_in[j].astype(BF16)
            o1, o2, o3, o4 = A_Q, A_Q + A_KV, A_Q + 2 * A_KV, A_Q + 2 * A_KV + A_IQ
            o5 = o4 + IDX_DIM
            wik = jnp.pad(w[:, o4:o5], ((0, 0), (0, LANES - IDX_DIM)))
            wiwt = jnp.pad(w[:, o5:].T, ((0, 16 - IDX_HEADS), (0, 0)))
            q, k, vt, iq, ik, iwt = _dsa_proj(h, pos, inv, w[:, :o1], w[:, o1:o2], w[:, o2:o3].T,
                                              w[:, o3:o4], wik, wiwt, tm=tm, tkv=tq)
            y = _dsa_attn(q, k, vt, iq, ik, iwt, batch=batch, seq=seq, tq=tq)
            w_o = a_w_o[j].astype(BF16)
        else:
            w = b_w_in[j].astype(BF16)
            o1, o2, o3, o4 = B_QK, 2 * B_QK, 2 * B_QK + B_V, 2 * B_QK + 2 * B_V
            wa = jnp.pad(w[:, o4:], ((0, 0), (0, LANES - B_GATE_RANK)))
            q, k, v, r, a = _gla_proj(h, w[:, :o1], w[:, o1:o2], w[:, o2:o3], w[:, o3:o4], wa, tm=tm)
            y = _gla(q, k, v, r, a, b_w_a2[j], row(b_b_a[j]), row(b_g_norm[j]),
                     batch=batch, seq=seq, rows=rows)
            w_o = b_w_o[j].astype(BF16)
        h = _out_ln(y, w_o, h, row(ln_mix_g[layer]), row(ln_mix_b[layer]), tm=tm)
        h = _mlp(h, mlp_w_up[layer].astype(BF16), mlp_w_down[layer].astype(BF16),
                 row(ln_mlp_g[layer]), row(ln_mlp_b[layer]), tm=tm, tf=1024)
    return h.reshape(batch, seq, D_MODEL)


_forward_jit = jax.jit(_forward)


def kernel(x, positions, a_w_in, a_w_o, b_w_in, b_w_a2, b_b_a, b_g_norm, b_w_o,
           ln_mix_g, ln_mix_b, mlp_w_up, mlp_w_down, ln_mlp_g, ln_mlp_b):
    return _forward_jit(x, positions, a_w_in, a_w_o, b_w_in, b_w_a2, b_b_a, b_g_norm, b_w_o,
                        ln_mix_g, ln_mix_b, mlp_w_up, mlp_w_down, ln_mlp_g, ln_mlp_b)
```

```python
import functools

import jax
import jax.numpy as jnp
from jax import lax
from jax.experimental import pallas as pl
from jax.experimental.pallas import tpu as pltpu

D_MODEL = 1024
DEPTH = 4

A_HEADS = 16
A_KV_HEADS = 4
A_HEAD_DIM = 64
A_REP = A_HEADS // A_KV_HEADS
IDX_HEADS = 8
IDX_DIM = 64
IDX_TOPK_MAX = 256
A_Q = A_HEADS * A_HEAD_DIM
A_KV = A_KV_HEADS * A_HEAD_DIM
A_IQ = IDX_HEADS * IDX_DIM

B_HEADS = 4
B_DK = D_MODEL // 2 // B_HEADS
B_DV = D_MODEL // B_HEADS
B_GATE_RANK = 16
B_GATE_TAU = 16.0
B_CHUNK = 64
B_QK = B_HEADS * B_DK
B_V = B_HEADS * B_DV

D_FF = 4 * D_MODEL

ROPE_THETA = 500000.0
ROT_DIM = A_HEAD_DIM // 4
ROT_HALF = ROT_DIM // 2
LN_EPS = 1e-5
RMS_EPS = 1e-6
DN_ALPHA = (2 * DEPTH) ** 0.25

LANES = 128
VMEM_LIMIT_BYTES = 56 * 1024 * 1024
INT32_MIN = -(2 ** 31)
INT32_MAX = 2 ** 31 - 1
NEG_BIG = -(2.0 ** 100)
EXP_CLAMP = 80.0
LOG2_E = 1.4426950408889634
PV_ROWS = A_HEAD_DIM + 16
SEARCH_VALUE_ROUNDS = 24
FLASH_HEADS = 4

BF16 = jnp.bfloat16
F32 = jnp.float32

_NT = (((1,), (1,)), ((), ()))
_TN = (((0,), (0,)), ((), ()))


def _params(*sem):
    return pltpu.CompilerParams(dimension_semantics=sem,
                                vmem_limit_bytes=VMEM_LIMIT_BYTES)


def _const_spec(shape):
    nd = len(shape)
    return pl.BlockSpec(shape, lambda *_: (0,) * nd)


def _float_to_key(x):
    bits = lax.bitcast_convert_type(x, jnp.int32)
    return bits ^ ((bits >> 31) & INT32_MAX)


def _key_to_float(key):
    return lax.bitcast_convert_type(key ^ ((key >> 31) & INT32_MAX), F32)


def _layer_norm(z, g, b):
    mu = jnp.mean(z, axis=-1, keepdims=True)
    zc = z - mu
    var = jnp.mean(zc * zc, axis=-1, keepdims=True)
    return zc * lax.rsqrt(var + LN_EPS) * g + b


def _rot_tables(pos, inv):
    ang = pos * inv
    c = jnp.cos(ang)
    s = jnp.sin(ang)
    lane = lax.broadcasted_iota(jnp.int32, ang.shape, 1) % A_HEAD_DIM
    s_lo = jnp.where(lane < ROT_HALF, -s, 0.0)
    s_hi = jnp.where((lane >= ROT_HALF) & (lane < ROT_DIM), s, 0.0)
    return c, s_lo, s_hi


def _rotate(t, c, s_lo, s_hi):
    w = t.shape[1]
    reps = w // LANES
    if reps > 1:
        c = jnp.concatenate([c] * reps, axis=1)
        s_lo = jnp.concatenate([s_lo] * reps, axis=1)
        s_hi = jnp.concatenate([s_hi] * reps, axis=1)
    up = pltpu.roll(t, w - ROT_HALF, axis=1)
    dn = pltpu.roll(t, ROT_HALF, axis=1)
    return t * c + up * s_lo + dn * s_hi


def _dsa_proj_kernel(h_ref, pos_ref, inv_ref, wq_ref, wk_ref, wvt_ref, wiq_ref, wik_ref, wiwt_ref,
                     q_ref, k_ref, vt_ref, iq_ref, ik_ref, iwt_ref, *, tkv):
    xb = h_ref[...].astype(BF16)
    tm = xb.shape[0]
    c, s_lo, s_hi = _rot_tables(pos_ref[...], inv_ref[...])

    q = jnp.dot(xb, wq_ref[...], preferred_element_type=F32)
    q = _rotate(q, c, s_lo, s_hi) * (A_HEAD_DIM ** -0.5 * LOG2_E)
    for hh in range(A_HEADS):
        q_ref[hh] = q[:, hh * A_HEAD_DIM:(hh + 1) * A_HEAD_DIM].astype(BF16)

    k = jnp.dot(xb, wk_ref[...], preferred_element_type=F32)
    k = _rotate(k, c, s_lo, s_hi)
    vt = lax.dot_general(wvt_ref[...], xb, _NT, preferred_element_type=F32)
    ones_row = jnp.where(lax.broadcasted_iota(jnp.int32, (PV_ROWS - A_HEAD_DIM, tkv), 0) == 0, 1.0, 0.0)
    for g in range(A_KV_HEADS):
        k_ref[g] = k[:, g * A_HEAD_DIM:(g + 1) * A_HEAD_DIM].astype(BF16)
        for j in range(tm // tkv):
            vt_gj = vt[g * A_HEAD_DIM:(g + 1) * A_HEAD_DIM, j * tkv:(j + 1) * tkv]
            vt_ref[g, j] = jnp.concatenate([vt_gj, ones_row], axis=0).astype(BF16)

    iq = jnp.dot(xb, wiq_ref[...], preferred_element_type=F32)
    iq = _rotate(iq, c, s_lo, s_hi) * (IDX_DIM ** -0.5)
    for hh in range(IDX_HEADS):
        iq_ref[hh] = iq[:, hh * IDX_DIM:(hh + 1) * IDX_DIM].astype(BF16)

    x = jnp.dot(xb, wik_ref[...], preferred_element_type=F32)
    first = lax.broadcasted_iota(jnp.int32, x.shape, 1) < IDX_DIM
    x = _rotate(x, jnp.where(first, c, 1.0), jnp.where(first, s_lo, 0.0), jnp.where(first, s_hi, 0.0))
    ik_ref[...] = x[:, :IDX_DIM].astype(BF16)
    iwt = lax.dot_general(wiwt_ref[...], xb, _NT, preferred_element_type=F32)
    iwt_ref[...] = iwt[:IDX_HEADS] * (IDX_HEADS ** -0.5)


def _dsa_proj(h, pos, inv, wq, wk, wvt, wiq, wik, wiwt, *, tm, tkv):
    t = h.shape[0]
    row = lambda i: (i, 0)
    head = lambda i: (0, i, 0)
    return pl.pallas_call(
        functools.partial(_dsa_proj_kernel, tkv=tkv),
        grid=(t // tm,),
        in_specs=[pl.BlockSpec((tm, D_MODEL), row), pl.BlockSpec((tm, 1), row),
                  _const_spec(inv.shape), _const_spec(wq.shape), _const_spec(wk.shape),
                  _const_spec(wvt.shape), _const_spec(wiq.shape), _const_spec(wik.shape),
                  _const_spec(wiwt.shape)],
        out_specs=[pl.BlockSpec((A_HEADS, tm, A_HEAD_DIM), head),
                   pl.BlockSpec((A_KV_HEADS, tm, A_HEAD_DIM), head),
                   pl.BlockSpec((A_KV_HEADS, tm // tkv, PV_ROWS, tkv), lambda i: (0, i, 0, 0)),
                   pl.BlockSpec((IDX_HEADS, tm, IDX_DIM), head),
                   pl.BlockSpec((tm, IDX_DIM), row),
                   pl.BlockSpec((IDX_HEADS, tm), lambda i: (0, i))],
        out_shape=[jax.ShapeDtypeStruct((A_HEADS, t, A_HEAD_DIM), BF16),
                   jax.ShapeDtypeStruct((A_KV_HEADS, t, A_HEAD_DIM), BF16),
                   jax.ShapeDtypeStruct((A_KV_HEADS, t // tkv, PV_ROWS, tkv), BF16),
                   jax.ShapeDtypeStruct((IDX_HEADS, t, IDX_DIM), BF16),
                   jax.ShapeDtypeStruct((t, IDX_DIM), BF16),
                   jax.ShapeDtypeStruct((IDX_HEADS, t), F32)],
        compiler_params=_params("parallel"),
        name="dsa_proj",
    )(h, pos, inv, wq, wk, wvt, wiq, wik, wiwt)


def _dsa_attn_kernel(q_ref, iq_ref, iwt_ref, k_ref, vt_ref, ik_ref, o_ref,
                     slab, m_sc, acc_sc, cut_sc, ot_sc, *, tq, top_k):
    i = pl.program_id(1)
    n_kt = i + 1
    q_pos = i * tq + lax.broadcasted_iota(jnp.int32, (1, tq), 1)
    key_off = lax.broadcasted_iota(jnp.int32, (tq, tq), 0)

    iw = iwt_ref[...]

    def fold(x, op):
        return op(x.reshape(tq // 8, 8, tq), axis=0)

    def score_tile(kt, carry):
        mx, mn, c0, c1 = carry
        ik_t = ik_ref[pl.ds(pl.multiple_of(kt * tq, tq), tq), :]
        sc = jnp.zeros((tq, tq), F32)
        for hh in range(IDX_HEADS):
            s = lax.dot_general(ik_t, iq_ref[hh], _NT, preferred_element_type=F32)
            sc = sc + iw[hh:hh + 1, :] * jnp.maximum(s, 0.0)
        key = _float_to_key(sc)
        causal = (kt * tq + key_off) <= q_pos
        key = jnp.where(causal, key, INT32_MIN)
        slab[kt] = key
        mx = jnp.maximum(mx, fold(key, jnp.max))
        mn = jnp.minimum(mn, fold(jnp.where(causal, key, INT32_MAX), jnp.min))
        c0 = c0 + fold(jnp.where(key >= 0, 1, 0), jnp.sum)
        c1 = c1 + fold(jnp.where(key >= 1, 1, 0), jnp.sum)
        return mx, mn, c0, c1

    init = (jnp.full((8, tq), INT32_MIN, jnp.int32), jnp.full((8, tq), INT32_MAX, jnp.int32),
            jnp.zeros((8, tq), jnp.int32), jnp.zeros((8, tq), jnp.int32))
    mx, mn, c0, c1 = lax.fori_loop(0, n_kt, score_tile, init)
    key_max = jnp.max(mx, axis=0, keepdims=True)
    key_min = jnp.min(mn, axis=0, keepdims=True)
    cnt_nonneg = jnp.sum(c0, axis=0, keepdims=True)
    cnt_pos = jnp.sum(c1, axis=0, keepdims=True)

    k_row = jnp.minimum(top_k, q_pos + 1)

    def count(pred):
        def body(kt, acc):
            hit = jnp.where(pred(slab[kt], kt * tq + key_off), 1, 0)
            return acc + fold(hit, jnp.sum)
        acc = lax.fori_loop(0, n_kt, body, jnp.zeros((8, tq), jnp.int32))
        return jnp.sum(acc, axis=0, keepdims=True)

    pos_case = cnt_pos >= k_row
    zero_case = jnp.logical_not(pos_case) & (cnt_nonneg >= k_row)
    lo0 = jnp.where(pos_case, 1, jnp.where(zero_case, 0, key_min))
    clo0 = jnp.where(pos_case, cnt_pos, jnp.where(zero_case, cnt_nonneg, q_pos + 1))
    hi0 = jnp.where(pos_case, key_max + 1, jnp.where(zero_case, 1, 0))
    chi0 = jnp.where(pos_case, 0, jnp.where(zero_case, cnt_pos, cnt_nonneg))

    def n_active(lo, hi, clo):
        return jnp.max(jnp.where((clo != k_row) & (hi - 1 > lo), 1, 0))

    def search_cond(carry):
        it, nact = carry[0], carry[1]
        return (nact > 0) & (it < SEARCH_VALUE_ROUNDS + 32)

    def search_body(carry):
        it, _, lo, hi, clo, chi = carry
        active = (clo != k_row) & (hi - 1 > lo)
        lov = _key_to_float(lo)
        hiv = _key_to_float(hi)
        frac_interp = ((clo - k_row).astype(F32) - 0.5) / (clo - chi).astype(F32)
        bisect = (it < 3) | (it % 3 == 2)
        frac = jnp.where(bisect, 0.5, frac_interp)
        cand_value = _float_to_key(lov + (hiv - lov) * frac)
        cand_key = lo + ((hi - lo) >> 1)
        cand = jnp.where(it >= SEARCH_VALUE_ROUNDS, cand_key, cand_value)
        cand = jnp.minimum(jnp.maximum(cand, lo + 1), hi - 1)
        cand = jnp.where(active, cand, lo)
        cnt = count(lambda ky, idx: ky >= cand)
        ok = cnt >= k_row
        lo = jnp.where(ok, cand, lo)
        clo = jnp.where(ok, cnt, clo)
        hi = jnp.where(ok, hi, cand)
        chi = jnp.where(ok, chi, cnt)
        return it + 1, n_active(lo, hi, clo), lo, hi, clo, chi

    _, _, thr, _, nge, _ = lax.while_loop(
        search_cond, search_body, (jnp.int32(0), n_active(lo0, hi0, clo0), lo0, hi0, clo0, chi0))

    cut_sc[...] = jnp.full((1, tq), n_kt * tq, jnp.int32)
    has_ties = jnp.max(nge - k_row) > 0

    @pl.when(has_ties)
    def _():
        n_gt = count(lambda ky, idx: ky > thr)
        need = (k_row - n_gt).astype(F32)
        tri = jnp.where(lax.broadcasted_iota(jnp.int32, (tq, tq), 1) <= key_off, 1.0, 0.0).astype(BF16)

        def tie_tile(kt, carry):
            seen, best = carry
            eq = slab[kt] == thr
            rank = jnp.dot(tri, jnp.where(eq, 1.0, 0.0).astype(BF16),
                           preferred_element_type=F32) + seen
            hit = eq & (rank == need)
            best = jnp.maximum(best, fold(jnp.where(hit, kt * tq + key_off, -1), jnp.max))
            return rank[tq - 1:tq, :], best

        _, best = lax.fori_loop(0, n_kt, tie_tile,
                                (jnp.zeros((1, tq), F32), jnp.full((8, tq), -1, jnp.int32)))
        cut_sc[...] = jnp.max(best, axis=0, keepdims=True)

    cut = cut_sc[...]

    m_sc[...] = jnp.full(m_sc.shape, NEG_BIG, F32)
    acc_sc[...] = jnp.zeros(acc_sc.shape, F32)

    n_steps = A_HEADS // FLASH_HEADS

    def mask_bias(kt):
        keys = slab[kt]
        sel = (keys > thr) | ((keys == thr) & ((kt * tq + key_off) <= cut))
        bias = jnp.where(sel, 0.0, NEG_BIG).astype(BF16)
        return jnp.concatenate([bias] * FLASH_HEADS, axis=1)

    def logits(kt, st, bias):
        qs = q_ref[st * FLASH_HEADS:(st + 1) * FLASH_HEADS].reshape(FLASH_HEADS * tq, A_HEAD_DIM)
        kg = k_ref[st * FLASH_HEADS // A_REP, pl.ds(pl.multiple_of(kt * tq, tq), tq), :]
        return lax.dot_general(kg, qs, _NT, preferred_element_type=F32).astype(BF16) + bias

    def attn_tile(kt, carry):
        s_next, bias = carry
        for st in range(n_steps):
            s = s_next
            if st + 1 < n_steps:
                s_next = logits(kt, st + 1, bias)
            else:
                kt_next = jnp.minimum(kt + 1, n_kt - 1)
                bias = mask_bias(kt_next)
                s_next = logits(kt_next, 0, bias)
            m_prev = m_sc[st]
            m_new = jnp.maximum(m_prev, jnp.max(s, axis=0, keepdims=True).astype(F32))
            alpha = jnp.exp2(m_prev - m_new)
            p = jnp.exp2(s - m_new.astype(BF16))
            acc_sc[st] = alpha * acc_sc[st] + jnp.dot(vt_ref[st * FLASH_HEADS // A_REP, kt], p,
                                                      preferred_element_type=F32)
            m_sc[st] = m_new
        return s_next, bias

    bias0 = mask_bias(0)
    lax.fori_loop(0, n_kt, attn_tile, (logits(0, 0, bias0), bias0))

    for st in range(n_steps):
        acc = acc_sc[st]
        og = acc[:A_HEAD_DIM] / acc[A_HEAD_DIM:A_HEAD_DIM + 1]
        for r in range(FLASH_HEADS):
            hh = st * FLASH_HEADS + r
            ot_sc[hh * A_HEAD_DIM:(hh + 1) * A_HEAD_DIM, :] = og[:, r * tq:(r + 1) * tq]
    o_ref[...] = ot_sc[...].T.astype(BF16)


def _dsa_attn(q, k, vt, iq, ik, iwt, *, batch, seq, tq):
    t = batch * seq
    nq = seq // tq
    top_k = min(IDX_TOPK_MAX, seq // 4)
    qtile = lambda b, i: (0, b * nq + i, 0)
    whole = lambda b, i: (0, b, 0)
    once = pl.Buffered(1)
    return pl.pallas_call(
        functools.partial(_dsa_attn_kernel, tq=tq, top_k=top_k),
        grid=(batch, nq),
        in_specs=[pl.BlockSpec((A_HEADS, tq, A_HEAD_DIM), qtile),
                  pl.BlockSpec((IDX_HEADS, tq, IDX_DIM), qtile),
                  pl.BlockSpec((IDX_HEADS, tq), lambda b, i: (0, b * nq + i)),
                  pl.BlockSpec((A_KV_HEADS, seq, A_HEAD_DIM), whole, pipeline_mode=once),
                  pl.BlockSpec((A_KV_HEADS, nq, PV_ROWS, tq), lambda b, i: (0, b, 0, 0),
                               pipeline_mode=once),
                  pl.BlockSpec((seq, IDX_DIM), lambda b, i: (b, 0), pipeline_mode=once)],
        out_specs=pl.BlockSpec((tq, A_Q), lambda b, i: (b * nq + i, 0)),
        out_shape=jax.ShapeDtypeStruct((t, A_Q), BF16),
        scratch_shapes=[pltpu.VMEM((nq, tq, tq), jnp.int32),
                        pltpu.VMEM((A_HEADS // FLASH_HEADS, 1, FLASH_HEADS * tq), F32),
                        pltpu.VMEM((A_HEADS // FLASH_HEADS, PV_ROWS, FLASH_HEADS * tq), F32),
                        pltpu.VMEM((1, tq), jnp.int32),
                        pltpu.VMEM((A_Q, tq), F32)],
        compiler_params=_params("parallel", "arbitrary"),
        name="dsa_attn",
    )(q, iq, iwt, k, vt, ik)


def _out_ln_kernel(y_ref, w_ref, h_ref, g_ref, b_ref, o_ref):
    mix = jnp.dot(y_ref[...], w_ref[...], preferred_element_type=F32)
    o_ref[...] = _layer_norm(DN_ALPHA * h_ref[...] + mix, g_ref[...], b_ref[...])


def _out_ln(y, w, h, g, b, *, tm):
    t, kdim = y.shape
    row = lambda i: (i, 0)
    return pl.pallas_call(
        _out_ln_kernel,
        grid=(t // tm,),
        in_specs=[pl.BlockSpec((tm, kdim), row), _const_spec(w.shape),
                  pl.BlockSpec((tm, D_MODEL), row), _const_spec(g.shape), _const_spec(b.shape)],
        out_specs=pl.BlockSpec((tm, D_MODEL), row),
        out_shape=jax.ShapeDtypeStruct((t, D_MODEL), F32),
        compiler_params=_params("parallel"),
        name="out_ln",
    )(y, w, h, g, b)


def _mlp_kernel(h_ref, wu_ref, wd_ref, g_ref, b_ref, o_ref, *, tf):
    h = h_ref[...]
    xb = h.astype(BF16)
    acc = jnp.zeros(h.shape, F32)
    for c in range(D_FF // tf):
        u = jnp.dot(xb, wu_ref[:, c * tf:(c + 1) * tf], preferred_element_type=F32)
        u = jnp.square(jnp.maximum(u, 0.0)).astype(BF16)
        acc = acc + jnp.dot(u, wd_ref[c * tf:(c + 1) * tf, :], preferred_element_type=F32)
    o_ref[...] = _layer_norm(DN_ALPHA * h + acc, g_ref[...], b_ref[...])


def _mlp(h, wu, wd, g, b, *, tm, tf):
    t = h.shape[0]
    row = lambda i: (i, 0)
    once = pl.Buffered(1)
    return pl.pallas_call(
        functools.partial(_mlp_kernel, tf=tf),
        grid=(t // tm,),
        in_specs=[pl.BlockSpec((tm, D_MODEL), row),
                  pl.BlockSpec(wu.shape, lambda i: (0, 0), pipeline_mode=once),
                  pl.BlockSpec(wd.shape, lambda i: (0, 0), pipeline_mode=once),
                  _const_spec(g.shape), _const_spec(b.shape)],
        out_specs=pl.BlockSpec((tm, D_MODEL), row),
        out_shape=jax.ShapeDtypeStruct((t, D_MODEL), F32),
        compiler_params=_params("parallel"),
        name="mlp_ln",
    )(h, wu, wd, g, b)


def _gla_proj_kernel(h_ref, wq_ref, wk_ref, wv_ref, wr_ref, wa_ref,
                     q_ref, k_ref, v_ref, r_ref, a_ref):
    xb = h_ref[...].astype(BF16)
    q = jnp.dot(xb, wq_ref[...], preferred_element_type=F32)
    q_ref[...] = (q * (B_DK ** -0.5)).astype(BF16)
    k_ref[...] = jnp.dot(xb, wk_ref[...], preferred_element_type=F32).astype(BF16)
    v_ref[...] = jnp.dot(xb, wv_ref[...], preferred_element_type=F32).astype(BF16)
    r_ref[...] = jnp.dot(xb, wr_ref[...], preferred_element_type=F32).astype(BF16)
    a = jnp.dot(xb, wa_ref[...], preferred_element_type=F32)
    a_ref[...] = a[:, :B_GATE_RANK]


def _gla_proj(h, wq, wk, wv, wr, wa, *, tm):
    t = h.shape[0]
    row = lambda i: (i, 0)
    return pl.pallas_call(
        _gla_proj_kernel,
        grid=(t // tm,),
        in_specs=[pl.BlockSpec((tm, D_MODEL), row), _const_spec(wq.shape), _const_spec(wk.shape),
                  _const_spec(wv.shape), _const_spec(wr.shape), _const_spec(wa.shape)],
        out_specs=[pl.BlockSpec((tm, B_QK), row), pl.BlockSpec((tm, B_QK), row),
                   pl.BlockSpec((tm, B_V), row), pl.BlockSpec((tm, B_V), row),
                   pl.BlockSpec((tm, B_GATE_RANK), row)],
        out_shape=[jax.ShapeDtypeStruct((t, B_QK), BF16), jax.ShapeDtypeStruct((t, B_QK), BF16),
                   jax.ShapeDtypeStruct((t, B_V), BF16), jax.ShapeDtypeStruct((t, B_V), BF16),
                   jax.ShapeDtypeStruct((t, B_GATE_RANK), F32)],
        compiler_params=_params("parallel"),
        name="gla_proj",
    )(h, wq, wk, wv, wr, wa)


def _gla_kernel(q_ref, k_ref, v_ref, r_ref, a_ref, w2_ref, ba_ref, gn_ref, y_ref, st_ref, *, rows):
    n_chunks = rows // B_CHUNK

    @pl.when(pl.program_id(2) == 0)
    def _():
        st_ref[...] = jnp.zeros(st_ref.shape, F32)

    x = jnp.dot(a_ref[...], w2_ref[...], preferred_element_type=F32,
                precision=lax.Precision.HIGHEST) + ba_ref[...]
    log_a = (jnp.minimum(x, 0.0) - jnp.log1p(jnp.exp(-jnp.abs(x)))) / B_GATE_TAU

    row_in_chunk = lax.broadcasted_iota(jnp.int32, (rows, B_DK), 0) % B_CHUNK
    b = log_a
    step = 1
    while step < B_CHUNK:
        b = b + jnp.where(row_in_chunk >= step, pltpu.roll(b, step, axis=0), 0.0)
        step *= 2

    b3 = b.reshape(n_chunks, B_CHUNK, B_DK)
    b_last = b3[:, B_CHUNK - 1:B_CHUNK, :]
    b_mid = b3[:, B_CHUNK // 2 - 1:B_CHUNK // 2, :]
    q3 = q_ref[...].astype(F32).reshape(n_chunks, B_CHUNK, B_DK)
    k3 = k_ref[...].astype(F32).reshape(n_chunks, B_CHUNK, B_DK)
    q_in = (q3 * jnp.exp(b3)).astype(BF16)
    k_out = (k3 * jnp.exp(b_last - b3)).astype(BF16)
    q_mid = (q3 * jnp.exp(jnp.minimum(b3 - b_mid, EXP_CLAMP))).astype(BF16)
    k_mid = (k3 * jnp.exp(jnp.minimum(b_mid - b3, EXP_CLAMP))).astype(BF16)
    decay = jnp.exp(b_last)

    tril = (lax.broadcasted_iota(jnp.int32, (B_CHUNK, B_CHUNK), 1)
            <= lax.broadcasted_iota(jnp.int32, (B_CHUNK, B_CHUNK), 0))
    gn = gn_ref[...]
    state = st_ref[...]
    for n in range(n_chunks):
        lo, hi = n * B_CHUNK, (n + 1) * B_CHUNK
        vn = v_ref[lo:hi, :]
        attn = lax.dot_general(q_mid[n], k_mid[n], _NT, preferred_element_type=F32)
        attn = jnp.where(tril, attn, 0.0).astype(BF16)
        o = jnp.dot(attn, vn, preferred_element_type=F32)
        o = o + lax.dot_general(q_in[n], state.astype(BF16), _NT, preferred_element_type=F32)
        upd = lax.dot_general(vn, k_out[n], _TN, preferred_element_type=F32)
        state = decay[n] * state + upd
        o = o * lax.rsqrt(jnp.mean(o * o, axis=-1, keepdims=True) + RMS_EPS) * gn
        rn = r_ref[lo:hi, :].astype(F32)
        gate = rn / (1.0 + jnp.exp(-rn))
        y_ref[lo:hi, :] = (o * gate).astype(BF16)
    st_ref[...] = state


def _gla(q, k, v, r, a, w2, ba, gn, *, batch, seq, rows):
    t = batch * seq
    nb = seq // rows
    tile = lambda b, h, j: (b * nb + j, h)
    return pl.pallas_call(
        functools.partial(_gla_kernel, rows=rows),
        grid=(batch, B_HEADS, nb),
        in_specs=[pl.BlockSpec((rows, B_DK), tile), pl.BlockSpec((rows, B_DK), tile),
                  pl.BlockSpec((rows, B_DV), tile), pl.BlockSpec((rows, B_DV), tile),
                  pl.BlockSpec((rows, B_GATE_RANK), lambda b, h, j: (b * nb + j, 0)),
                  pl.BlockSpec((B_GATE_RANK, B_DK), lambda b, h, j: (0, h)),
                  pl.BlockSpec((1, B_DK), lambda b, h, j: (0, h)),
                  _const_spec(gn.shape)],
        out_specs=pl.BlockSpec((rows, B_DV), tile),
        out_shape=jax.ShapeDtypeStruct((t, B_V), BF16),
        scratch_shapes=[pltpu.VMEM((B_DV, B_DK), F32)],
        compiler_params=_params("parallel", "parallel", "arbitrary"),
        name="gla",
    )(q, k, v, r, a, w2, ba, gn)


def _tile(n, pref):
    return pref if n % pref == 0 else n


def _forward(x, positions, a_w_in, a_w_o, b_w_in, b_w_a2, b_b_a, b_g_norm, b_w_o,
             ln_mix_g, ln_mix_b, mlp_w_up, mlp_w_down, ln_mlp_g, ln_mlp_b):
    batch, seq, _ = x.shape
    t = batch * seq
    tm = _tile(t, 512)
    tq = _tile(seq, 256)
    rows = _tile(seq, 512)

    h = x.reshape(t, D_MODEL)
    pos = positions.reshape(t, 1).astype(F32)
    inv = ROPE_THETA ** (-jnp.arange(0, ROT_DIM, 2, dtype=F32) / ROT_DIM)
    inv = jnp.concatenate([inv, inv, jnp.zeros((A_HEAD_DIM - ROT_DIM,), F32)])
    inv = jnp.tile(inv, LANES // A_HEAD_DIM).reshape(1, LANES)
    row = lambda p: p.reshape(1, -1)

    for layer in range(DEPTH):
        j = layer // 2
        if layer % 2 == 0:
            w = a_w_in[j].astype(BF16)
            o1, o2, o3, o4 = A_Q, A_Q + A_KV, A_Q + 2 * A_KV, A_Q + 2 * A_KV + A_IQ
            o5 = o4 + IDX_DIM
            wik = jnp.pad(w[:, o4:o5], ((0, 0), (0, LANES - IDX_DIM)))
            wiwt = jnp.pad(w[:, o5:].T, ((0, 16 - IDX_HEADS), (0, 0)))
            q, k, vt, iq, ik, iwt = _dsa_proj(h, pos, inv, w[:, :o1], w[:, o1:o2], w[:, o2:o3].T,
                                              w[:, o3:o4], wik, wiwt, tm=tm, tkv=tq)
            y = _dsa_attn(q, k, vt, iq, ik, iwt, batch=batch, seq=seq, tq=tq)
            w_o = a_w_o[j].astype(BF16)
        else:
            w = b_w_in[j].astype(BF16)
            o1, o2, o3, o4 = B_QK, 2 * B_QK, 2 * B_QK + B_V, 2 * B_QK + 2 * B_V
            wa = jnp.pad(w[:, o4:], ((0, 0), (0, LANES - B_GATE_RANK)))
            q, k, v, r, a = _gla_proj(h, w[:, :o1], w[:, o1:o2], w[:, o2:o3], w[:, o3:o4], wa, tm=tm)
            y = _gla(q, k, v, r, a, b_w_a2[j], row(b_b_a[j]), row(b_g_norm[j]),
                     batch=batch, seq=seq, rows=rows)
            w_o = b_w_o[j].astype(BF16)
        h = _out_ln(y, w_o, h, row(ln_mix_g[layer]), row(ln_mix_b[layer]), tm=tm)
        h = _mlp(h, mlp_w_up[layer].astype(BF16), mlp_w_down[layer].astype(BF16),
                 row(ln_mlp_g[layer]), row(ln_mlp_b[layer]), tm=tm, tf=1024)
    return h.reshape(batch, seq, D_MODEL)


_forward_jit = jax.jit(_forward)


def kernel(x, positions, a_w_in, a_w_o, b_w_in, b_w_a2, b_b_a, b_g_norm, b_w_o,
           ln_mix_g, ln_mix_b, mlp_w_up, mlp_w_down, ln_mlp_g, ln_mlp_b):
    return _forward_jit(x, positions, a_w_in, a_w_o, b_w_in, b_w_a2, b_b_a, b_g_norm, b_w_o,
                        ln_mix_g, ln_mix_b, mlp_w_up, mlp_w_down, ln_mlp_g, ln_mlp_b)
```

```python
import functools

import jax
import jax.numpy as jnp
from jax import lax
from jax.experimental import pallas as pl
from jax.experimental.pallas import tpu as pltpu

D_MODEL = 1024
DEPTH = 4

A_HEADS = 16
A_KV_HEADS = 4
A_HEAD_DIM = 64
A_REP = A_HEADS // A_KV_HEADS
IDX_HEADS = 8
IDX_DIM = 64
IDX_TOPK_MAX = 256
A_Q = A_HEADS * A_HEAD_DIM
A_KV = A_KV_HEADS * A_HEAD_DIM
A_IQ = IDX_HEADS * IDX_DIM

B_HEADS = 4
B_DK = D_MODEL // 2 // B_HEADS
B_DV = D_MODEL // B_HEADS
B_GATE_RANK = 16
B_GATE_TAU = 16.0
B_CHUNK = 64
B_QK = B_HEADS * B_DK
B_V = B_HEADS * B_DV

D_FF = 4 * D_MODEL

ROPE_THETA = 500000.0
ROT_DIM = A_HEAD_DIM // 4
ROT_HALF = ROT_DIM // 2
LN_EPS = 1e-5
RMS_EPS = 1e-6
DN_ALPHA = (2 * DEPTH) ** 0.25

LANES = 128
VMEM_LIMIT_BYTES = 56 * 1024 * 1024
INT32_MIN = -(2 ** 31)
INT32_MAX = 2 ** 31 - 1
NEG_BIG = -(2.0 ** 100)
EXP_CLAMP = 80.0
LOG2_E = 1.4426950408889634
PV_ROWS = A_HEAD_DIM + 16
SEARCH_VALUE_ROUNDS = 24
FLASH_HEADS = 4
GLA_HEADS_PER_STEP = 4
FOLD_ROWS = 32

BF16 = jnp.bfloat16
F32 = jnp.float32

_NT = (((1,), (1,)), ((), ()))
_TN = (((0,), (0,)), ((), ()))


def _params(*sem):
    return pltpu.CompilerParams(dimension_semantics=sem,
                                vmem_limit_bytes=VMEM_LIMIT_BYTES)


def _const_spec(shape):
    nd = len(shape)
    return pl.BlockSpec(shape, lambda *_: (0,) * nd)


def _float_to_key(x):
    bits = lax.bitcast_convert_type(x, jnp.int32)
    return bits ^ ((bits >> 31) & INT32_MAX)


def _key_to_float(key):
    return lax.bitcast_convert_type(key ^ ((key >> 31) & INT32_MAX), F32)


def _layer_norm(z, g, b):
    mu = jnp.mean(z, axis=-1, keepdims=True)
    zc = z - mu
    var = jnp.mean(zc * zc, axis=-1, keepdims=True)
    return zc * lax.rsqrt(var + LN_EPS) * g + b


def _rot_tables(pos, inv):
    ang = pos * inv
    c = jnp.cos(ang)
    s = jnp.sin(ang)
    lane = lax.broadcasted_iota(jnp.int32, ang.shape, 1) % A_HEAD_DIM
    s_lo = jnp.where(lane < ROT_HALF, -s, 0.0)
    s_hi = jnp.where((lane >= ROT_HALF) & (lane < ROT_DIM), s, 0.0)
    return c, s_lo, s_hi


def _rotate(t, c, s_lo, s_hi):
    w = t.shape[1]
    reps = w // LANES
    if reps > 1:
        c = jnp.concatenate([c] * reps, axis=1)
        s_lo = jnp.concatenate([s_lo] * reps, axis=1)
        s_hi = jnp.concatenate([s_hi] * reps, axis=1)
    up = pltpu.roll(t, w - ROT_HALF, axis=1)
    dn = pltpu.roll(t, ROT_HALF, axis=1)
    return t * c + up * s_lo + dn * s_hi


def _dsa_proj_kernel(h_ref, pos_ref, inv_ref, wq_ref, wk_ref, wvt_ref, wiq_ref, wik_ref, wiwt_ref,
                     q_ref, k_ref, vt_ref, iq_ref, ik_ref, iwt_ref, *, tkv):
    xb = h_ref[...].astype(BF16)
    tm = xb.shape[0]
    c, s_lo, s_hi = _rot_tables(pos_ref[...], inv_ref[...])

    q = jnp.dot(xb, wq_ref[...], preferred_element_type=F32)
    q = _rotate(q, c, s_lo, s_hi) * (A_HEAD_DIM ** -0.5 * LOG2_E)
    for hh in range(A_HEADS):
        q_ref[hh] = q[:, hh * A_HEAD_DIM:(hh + 1) * A_HEAD_DIM].astype(BF16)

    k = jnp.dot(xb, wk_ref[...], preferred_element_type=F32)
    k = _rotate(k, c, s_lo, s_hi)
    vt = lax.dot_general(wvt_ref[...], xb, _NT, preferred_element_type=F32)
    ones_row = jnp.where(lax.broadcasted_iota(jnp.int32, (PV_ROWS - A_HEAD_DIM, tkv), 0) == 0, 1.0, 0.0)
    for g in range(A_KV_HEADS):
        k_ref[g] = k[:, g * A_HEAD_DIM:(g + 1) * A_HEAD_DIM].astype(BF16)
        for j in range(tm // tkv):
            vt_gj = vt[g * A_HEAD_DIM:(g + 1) * A_HEAD_DIM, j * tkv:(j + 1) * tkv]
            vt_ref[g, j] = jnp.concatenate([vt_gj, ones_row], axis=0).astype(BF16)

    iq = jnp.dot(xb, wiq_ref[...], preferred_element_type=F32)
    iq = _rotate(iq, c, s_lo, s_hi) * (IDX_DIM ** -0.5)
    for hh in range(IDX_HEADS):
        iq_ref[hh] = iq[:, hh * IDX_DIM:(hh + 1) * IDX_DIM].astype(BF16)

    x = jnp.dot(xb, wik_ref[...], preferred_element_type=F32)
    first = lax.broadcasted_iota(jnp.int32, x.shape, 1) < IDX_DIM
    x = _rotate(x, jnp.where(first, c, 1.0), jnp.where(first, s_lo, 0.0), jnp.where(first, s_hi, 0.0))
    ik_ref[...] = x[:, :IDX_DIM].astype(BF16)
    iwt = lax.dot_general(wiwt_ref[...], xb, _NT, preferred_element_type=F32)
    iwt_ref[...] = iwt[:IDX_HEADS] * (IDX_HEADS ** -0.5)


def _dsa_proj(h, pos, inv, wq, wk, wvt, wiq, wik, wiwt, *, tm, tkv):
    t = h.shape[0]
    row = lambda i: (i, 0)
    head = lambda i: (0, i, 0)
    return pl.pallas_call(
        functools.partial(_dsa_proj_kernel, tkv=tkv),
        grid=(t // tm,),
        in_specs=[pl.BlockSpec((tm, D_MODEL), row), pl.BlockSpec((tm, 1), row),
                  _const_spec(inv.shape), _const_spec(wq.shape), _const_spec(wk.shape),
                  _const_spec(wvt.shape), _const_spec(wiq.shape), _const_spec(wik.shape),
                  _const_spec(wiwt.shape)],
        out_specs=[pl.BlockSpec((A_HEADS, tm, A_HEAD_DIM), head),
                   pl.BlockSpec((A_KV_HEADS, tm, A_HEAD_DIM), head),
                   pl.BlockSpec((A_KV_HEADS, tm // tkv, PV_ROWS, tkv), lambda i: (0, i, 0, 0)),
                   pl.BlockSpec((IDX_HEADS, tm, IDX_DIM), head),
                   pl.BlockSpec((tm, IDX_DIM), row),
                   pl.BlockSpec((IDX_HEADS, tm), lambda i: (0, i))],
        out_shape=[jax.ShapeDtypeStruct((A_HEADS, t, A_HEAD_DIM), BF16),
                   jax.ShapeDtypeStruct((A_KV_HEADS, t, A_HEAD_DIM), BF16),
                   jax.ShapeDtypeStruct((A_KV_HEADS, t // tkv, PV_ROWS, tkv), BF16),
                   jax.ShapeDtypeStruct((IDX_HEADS, t, IDX_DIM), BF16),
                   jax.ShapeDtypeStruct((t, IDX_DIM), BF16),
                   jax.ShapeDtypeStruct((IDX_HEADS, t), F32)],
        compiler_params=_params("parallel"),
        name="dsa_proj",
    )(h, pos, inv, wq, wk, wvt, wiq, wik, wiwt)


def _dsa_attn_kernel(q_ref, iq_ref, iwt_ref, k_ref, vt_ref, ik_ref, o_ref,
                     slab, m_sc, acc_sc, cut_sc, ot_sc, *, tq, top_k):
    i = pl.program_id(1)
    n_kt = i + 1
    q_pos = i * tq + lax.broadcasted_iota(jnp.int32, (1, tq), 1)
    key_off = lax.broadcasted_iota(jnp.int32, (tq, tq), 0)

    iw = iwt_ref[...]

    def fold(x, op):
        return op(x.reshape(-1, FOLD_ROWS, tq), axis=0)

    def score_tile(kt, carry):
        mx, mn, c0, c1 = carry
        ik_t = ik_ref[pl.ds(pl.multiple_of(kt * tq, tq), tq), :]
        sc = jnp.zeros((tq, tq), F32)
        for hh in range(IDX_HEADS):
            s = lax.dot_general(ik_t, iq_ref[hh], _NT, preferred_element_type=F32)
            sc = sc + iw[hh:hh + 1, :] * jnp.maximum(s, 0.0)
        key = _float_to_key(sc)
        causal = (kt * tq + key_off) <= q_pos
        key = jnp.where(causal, key, INT32_MIN)
        slab[kt] = key
        mx = jnp.maximum(mx, fold(key, jnp.max))
        mn = jnp.minimum(mn, fold(jnp.where(causal, key, INT32_MAX), jnp.min))
        c0 = c0 + fold(jnp.where(key >= 0, 1, 0), jnp.sum)
        c1 = c1 + fold(jnp.where(key >= 1, 1, 0), jnp.sum)
        return mx, mn, c0, c1

    part = (FOLD_ROWS, tq)
    init = (jnp.full(part, INT32_MIN, jnp.int32), jnp.full(part, INT32_MAX, jnp.int32),
            jnp.zeros(part, jnp.int32), jnp.zeros(part, jnp.int32))
    mx, mn, c0, c1 = lax.fori_loop(0, n_kt, score_tile, init)
    key_max = jnp.max(mx, axis=0, keepdims=True)
    key_min = jnp.min(mn, axis=0, keepdims=True)
    cnt_nonneg = jnp.sum(c0, axis=0, keepdims=True)
    cnt_pos = jnp.sum(c1, axis=0, keepdims=True)

    k_row = jnp.minimum(top_k, q_pos + 1)

    @pl.when(n_kt % 2 == 1)
    def _():
        slab[n_kt] = jnp.full((tq, tq), INT32_MIN, jnp.int32)

    def count(pred):
        def body(j, acc):
            hit = jnp.where(pred(slab[pl.ds(2 * j, 2)]), 1, 0)
            return acc + fold(hit, jnp.sum)
        acc = lax.fori_loop(0, (n_kt + 1) // 2, body, jnp.zeros(part, jnp.int32))
        return jnp.sum(acc, axis=0, keepdims=True)

    pos_case = cnt_pos >= k_row
    zero_case = jnp.logical_not(pos_case) & (cnt_nonneg >= k_row)
    lo0 = jnp.where(pos_case, 1, jnp.where(zero_case, 0, key_min))
    clo0 = jnp.where(pos_case, cnt_pos, jnp.where(zero_case, cnt_nonneg, q_pos + 1))
    hi0 = jnp.where(pos_case, key_max + 1, jnp.where(zero_case, 1, 0))
    chi0 = jnp.where(pos_case, 0, jnp.where(zero_case, cnt_pos, cnt_nonneg))

    def n_active(lo, hi, clo):
        return jnp.max(jnp.where((clo != k_row) & (hi - 1 > lo), 1, 0))

    def search_cond(carry):
        it, nact = carry[0], carry[1]
        return (nact > 0) & (it < SEARCH_VALUE_ROUNDS + 32)

    def search_body(carry):
        it, _, lo, hi, clo, chi = carry
        active = (clo != k_row) & (hi - 1 > lo)
        lov = _key_to_float(lo)
        hiv = _key_to_float(hi)
        frac_interp = ((clo - k_row).astype(F32) - 0.5) / (clo - chi).astype(F32)
        bisect = (it < 3) | (it % 3 == 2)
        frac = jnp.where(bisect, 0.5, frac_interp)
        cand_value = _float_to_key(lov + (hiv - lov) * frac)
        cand_key = lo + ((hi - lo) >> 1)
        cand = jnp.where(it >= SEARCH_VALUE_ROUNDS, cand_key, cand_value)
        cand = jnp.minimum(jnp.maximum(cand, lo + 1), hi - 1)
        cand = jnp.where(active, cand, lo)
        cnt = count(lambda ky: ky >= cand)
        ok = cnt >= k_row
        lo = jnp.where(ok, cand, lo)
        clo = jnp.where(ok, cnt, clo)
        hi = jnp.where(ok, hi, cand)
        chi = jnp.where(ok, chi, cnt)
        return it + 1, n_active(lo, hi, clo), lo, hi, clo, chi

    _, _, thr, _, nge, _ = lax.while_loop(
        search_cond, search_body, (jnp.int32(0), n_active(lo0, hi0, clo0), lo0, hi0, clo0, chi0))

    cut_sc[...] = jnp.full((1, tq), n_kt * tq, jnp.int32)
    has_ties = jnp.max(nge - k_row) > 0

    @pl.when(has_ties)
    def _():
        n_gt = count(lambda ky: ky > thr)
        need = (k_row - n_gt).astype(F32)
        tri = jnp.where(lax.broadcasted_iota(jnp.int32, (tq, tq), 1) <= key_off, 1.0, 0.0).astype(BF16)

        def tie_tile(kt, carry):
            seen, best = carry
            eq = slab[kt] == thr
            rank = jnp.dot(tri, jnp.where(eq, 1.0, 0.0).astype(BF16),
                           preferred_element_type=F32) + seen
            hit = eq & (rank == need)
            best = jnp.maximum(best, fold(jnp.where(hit, kt * tq + key_off, -1), jnp.max))
            return rank[tq - 1:tq, :], best

        _, best = lax.fori_loop(0, n_kt, tie_tile,
                                (jnp.zeros((1, tq), F32), jnp.full(part, -1, jnp.int32)))
        cut_sc[...] = jnp.max(best, axis=0, keepdims=True)

    cut = cut_sc[...]

    m_sc[...] = jnp.full(m_sc.shape, NEG_BIG, F32)
    acc_sc[...] = jnp.zeros(acc_sc.shape, F32)

    n_steps = A_HEADS // FLASH_HEADS

    def mask_bias(kt):
        keys = slab[kt]
        sel = (keys > thr) | ((keys == thr) & ((kt * tq + key_off) <= cut))
        bias = jnp.where(sel, 0.0, NEG_BIG).astype(BF16)
        return jnp.concatenate([bias] * FLASH_HEADS, axis=1)

    def logits(kt, st, bias):
        qs = q_ref[st * FLASH_HEADS:(st + 1) * FLASH_HEADS].reshape(FLASH_HEADS * tq, A_HEAD_DIM)
        kg = k_ref[st * FLASH_HEADS // A_REP, pl.ds(pl.multiple_of(kt * tq, tq), tq), :]
        return lax.dot_general(kg, qs, _NT, preferred_element_type=F32).astype(BF16) + bias

    def attn_tile(kt, carry):
        s_next, bias = carry
        for st in range(n_steps):
            s = s_next
            if st + 1 < n_steps:
                s_next = logits(kt, st + 1, bias)
            else:
                kt_next = jnp.minimum(kt + 1, n_kt - 1)
                bias = mask_bias(kt_next)
                s_next = logits(kt_next, 0, bias)
            m_prev = m_sc[st]
            m_new = jnp.maximum(m_prev, jnp.max(s, axis=0, keepdims=True).astype(F32))
            alpha = jnp.exp2(m_prev - m_new)
            p = jnp.exp2(s - m_new.astype(BF16))
            acc_sc[st] = alpha * acc_sc[st] + jnp.dot(vt_ref[st * FLASH_HEADS // A_REP, kt], p,
                                                      preferred_element_type=F32)
            m_sc[st] = m_new
        return s_next, bias

    bias0 = mask_bias(0)
    lax.fori_loop(0, n_kt, attn_tile, (logits(0, 0, bias0), bias0))

    for st in range(n_steps):
        acc = acc_sc[st]
        og = acc[:A_HEAD_DIM] / acc[A_HEAD_DIM:A_HEAD_DIM + 1]
        for r in range(FLASH_HEADS):
            hh = st * FLASH_HEADS + r
            ot_sc[hh * A_HEAD_DIM:(hh + 1) * A_HEAD_DIM, :] = og[:, r * tq:(r + 1) * tq]
    o_ref[...] = ot_sc[...].T.astype(BF16)


def _dsa_attn(q, k, vt, iq, ik, iwt, *, batch, seq, tq):
    t = batch * seq
    nq = seq // tq
    top_k = min(IDX_TOPK_MAX, seq // 4)
    qtile = lambda b, i: (0, b * nq + i, 0)
    whole = lambda b, i: (0, b, 0)
    once = pl.Buffered(1)
    return pl.pallas_call(
        functools.partial(_dsa_attn_kernel, tq=tq, top_k=top_k),
        grid=(batch, nq),
        in_specs=[pl.BlockSpec((A_HEADS, tq, A_HEAD_DIM), qtile),
                  pl.BlockSpec((IDX_HEADS, tq, IDX_DIM), qtile),
                  pl.BlockSpec((IDX_HEADS, tq), lambda b, i: (0, b * nq + i)),
                  pl.BlockSpec((A_KV_HEADS, seq, A_HEAD_DIM), whole, pipeline_mode=once),
                  pl.BlockSpec((A_KV_HEADS, nq, PV_ROWS, tq), lambda b, i: (0, b, 0, 0),
                               pipeline_mode=once),
                  pl.BlockSpec((seq, IDX_DIM), lambda b, i: (b, 0), pipeline_mode=once)],
        out_specs=pl.BlockSpec((tq, A_Q), lambda b, i: (b * nq + i, 0)),
        out_shape=jax.ShapeDtypeStruct((t, A_Q), BF16),
        scratch_shapes=[pltpu.VMEM((nq + nq % 2, tq, tq), jnp.int32),
                        pltpu.VMEM((A_HEADS // FLASH_HEADS, 1, FLASH_HEADS * tq), F32),
                        pltpu.VMEM((A_HEADS // FLASH_HEADS, PV_ROWS, FLASH_HEADS * tq), F32),
                        pltpu.VMEM((1, tq), jnp.int32),
                        pltpu.VMEM((A_Q, tq), F32)],
        compiler_params=_params("parallel", "arbitrary"),
        name="dsa_attn",
    )(q, iq, iwt, k, vt, ik)


def _mix_mlp_kernel(y_ref, wo_ref, h_ref, g1_ref, b1_ref, wu_ref, wd_ref, g2_ref, b2_ref, o_ref, *, tf):
    mix = jnp.dot(y_ref[...], wo_ref[...], preferred_element_type=F32)
    h = _layer_norm(DN_ALPHA * h_ref[...] + mix, g1_ref[...], b1_ref[...])
    xb = h.astype(BF16)
    acc = jnp.zeros(h.shape, F32)
    for c in range(D_FF // tf):
        u = jnp.dot(xb, wu_ref[:, c * tf:(c + 1) * tf], preferred_element_type=F32)
        u = jnp.square(jnp.maximum(u, 0.0)).astype(BF16)
        acc = acc + jnp.dot(u, wd_ref[c * tf:(c + 1) * tf, :], preferred_element_type=F32)
    o_ref[...] = _layer_norm(DN_ALPHA * h + acc, g2_ref[...], b2_ref[...])


def _mix_mlp(y, wo, h, g1, b1, wu, wd, g2, b2, *, tm, tf):
    t, kdim = y.shape
    row = lambda i: (i, 0)
    once = pl.Buffered(1)
    resident = lambda w: pl.BlockSpec(w.shape, lambda i: (0, 0), pipeline_mode=once)
    return pl.pallas_call(
        functools.partial(_mix_mlp_kernel, tf=tf),
        grid=(t // tm,),
        in_specs=[pl.BlockSpec((tm, kdim), row), resident(wo), pl.BlockSpec((tm, D_MODEL), row),
                  _const_spec(g1.shape), _const_spec(b1.shape), resident(wu), resident(wd),
                  _const_spec(g2.shape), _const_spec(b2.shape)],
        out_specs=pl.BlockSpec((tm, D_MODEL), row),
        out_shape=jax.ShapeDtypeStruct((t, D_MODEL), F32),
        compiler_params=_params("parallel"),
        name="mix_mlp",
    )(y, wo, h, g1, b1, wu, wd, g2, b2)


def _gla_proj_kernel(h_ref, wq_ref, wk_ref, wv_ref, wr_ref, wa_ref,
                     q_ref, k_ref, v_ref, r_ref, a_ref):
    xb = h_ref[...].astype(BF16)
    q = jnp.dot(xb, wq_ref[...], preferred_element_type=F32)
    q_ref[...] = (q * (B_DK ** -0.5)).astype(BF16)
    k_ref[...] = jnp.dot(xb, wk_ref[...], preferred_element_type=F32).astype(BF16)
    v_ref[...] = jnp.dot(xb, wv_ref[...], preferred_element_type=F32).astype(BF16)
    r_ref[...] = jnp.dot(xb, wr_ref[...], preferred_element_type=F32).astype(BF16)
    a = jnp.dot(xb, wa_ref[...], preferred_element_type=F32)
    a_ref[...] = a[:, :B_GATE_RANK]


def _gla_proj(h, wq, wk, wv, wr, wa, *, tm):
    t = h.shape[0]
    row = lambda i: (i, 0)
    return pl.pallas_call(
        _gla_proj_kernel,
        grid=(t // tm,),
        in_specs=[pl.BlockSpec((tm, D_MODEL), row), _const_spec(wq.shape), _const_spec(wk.shape),
                  _const_spec(wv.shape), _const_spec(wr.shape), _const_spec(wa.shape)],
        out_specs=[pl.BlockSpec((tm, B_QK), row), pl.BlockSpec((tm, B_QK), row),
                   pl.BlockSpec((tm, B_V), row), pl.BlockSpec((tm, B_V), row),
                   pl.BlockSpec((tm, B_GATE_RANK), row)],
        out_shape=[jax.ShapeDtypeStruct((t, B_QK), BF16), jax.ShapeDtypeStruct((t, B_QK), BF16),
                   jax.ShapeDtypeStruct((t, B_V), BF16), jax.ShapeDtypeStruct((t, B_V), BF16),
                   jax.ShapeDtypeStruct((t, B_GATE_RANK), F32)],
        compiler_params=_params("parallel"),
        name="gla_proj",
    )(h, wq, wk, wv, wr, wa)


def _gla_kernel(q_ref, k_ref, v_ref, r_ref, a_ref, w2_ref, ba_ref, gn_ref, y_ref, st_ref, *, rows, heads):
    n_chunks = rows // B_CHUNK

    @pl.when(pl.program_id(2) == 0)
    def _():
        st_ref[...] = jnp.zeros(st_ref.shape, F32)

    x = jnp.dot(a_ref[...], w2_ref[...], preferred_element_type=F32,
                precision=lax.Precision.HIGHEST) + ba_ref[...]
    log_a = (jnp.minimum(x, 0.0) - jnp.log1p(jnp.exp(-jnp.abs(x)))) / B_GATE_TAU

    row_in_chunk = lax.broadcasted_iota(jnp.int32, log_a.shape, 0) % B_CHUNK
    b = log_a
    step = 1
    while step < B_CHUNK:
        b = b + jnp.where(row_in_chunk >= step, pltpu.roll(b, step, axis=0), 0.0)
        step *= 2

    tril = (lax.broadcasted_iota(jnp.int32, (B_CHUNK, B_CHUNK), 1)
            <= lax.broadcasted_iota(jnp.int32, (B_CHUNK, B_CHUNK), 0))
    gn = gn_ref[...]

    prep = []
    for hd in range(heads):
        ks = slice(hd * B_DK, (hd + 1) * B_DK)
        b3 = b[:, ks].reshape(n_chunks, B_CHUNK, B_DK)
        b_last = b3[:, B_CHUNK - 1:B_CHUNK, :]
        b_mid = b3[:, B_CHUNK // 2 - 1:B_CHUNK // 2, :]
        q3 = q_ref[:, ks].astype(F32).reshape(n_chunks, B_CHUNK, B_DK)
        k3 = k_ref[:, ks].astype(F32).reshape(n_chunks, B_CHUNK, B_DK)
        prep.append(dict(
            q_in=(q3 * jnp.exp(b3)).astype(BF16),
            k_out=(k3 * jnp.exp(b_last - b3)).astype(BF16),
            q_mid=(q3 * jnp.exp(jnp.minimum(b3 - b_mid, EXP_CLAMP))).astype(BF16),
            k_mid=(k3 * jnp.exp(jnp.minimum(b_mid - b3, EXP_CLAMP))).astype(BF16),
            decay=jnp.exp(b_last),
            state=st_ref[hd]))

    for n in range(n_chunks):
        lo, hi = n * B_CHUNK, (n + 1) * B_CHUNK
        for hd in range(heads):
            p = prep[hd]
            vs = slice(hd * B_DV, (hd + 1) * B_DV)
            vn = v_ref[lo:hi, vs]
            attn = lax.dot_general(p["q_mid"][n], p["k_mid"][n], _NT, preferred_element_type=F32)
            attn = jnp.where(tril, attn, 0.0).astype(BF16)
            o = jnp.dot(attn, vn, preferred_element_type=F32)
            o = o + lax.dot_general(p["q_in"][n], p["state"].astype(BF16), _NT,
                                    preferred_element_type=F32)
            upd = lax.dot_general(vn, p["k_out"][n], _TN, preferred_element_type=F32)
            p["state"] = p["decay"][n] * p["state"] + upd
            o = o * lax.rsqrt(jnp.mean(o * o, axis=-1, keepdims=True) + RMS_EPS) * gn
            rn = r_ref[lo:hi, vs].astype(F32)
            gate = rn / (1.0 + jnp.exp(-rn))
            y_ref[lo:hi, vs] = (o * gate).astype(BF16)
    for hd in range(heads):
        st_ref[hd] = prep[hd]["state"]


def _gla(q, k, v, r, a, w2, ba, gn, *, batch, seq, rows, heads):
    t = batch * seq
    nb = seq // rows
    tile = lambda b, h, j: (b * nb + j, h)
    return pl.pallas_call(
        functools.partial(_gla_kernel, rows=rows, heads=heads),
        grid=(batch, B_HEADS // heads, nb),
        in_specs=[pl.BlockSpec((rows, heads * B_DK), tile), pl.BlockSpec((rows, heads * B_DK), tile),
                  pl.BlockSpec((rows, heads * B_DV), tile), pl.BlockSpec((rows, heads * B_DV), tile),
                  pl.BlockSpec((rows, B_GATE_RANK), lambda b, h, j: (b * nb + j, 0)),
                  pl.BlockSpec((B_GATE_RANK, heads * B_DK), lambda b, h, j: (0, h)),
                  pl.BlockSpec((1, heads * B_DK), lambda b, h, j: (0, h)),
                  _const_spec(gn.shape)],
        out_specs=pl.BlockSpec((rows, heads * B_DV), tile),
        out_shape=jax.ShapeDtypeStruct((t, B_V), BF16),
        scratch_shapes=[pltpu.VMEM((heads, B_DV, B_DK), F32)],
        compiler_params=_params("parallel", "parallel", "arbitrary"),
        name="gla",
    )(q, k, v, r, a, w2, ba, gn)


def _tile(n, pref):
    return pref if n % pref == 0 else n


def _forward(x, positions, a_w_in, a_w_o, b_w_in, b_w_a2, b_b_a, b_g_norm, b_w_o,
             ln_mix_g, ln_mix_b, mlp_w_up, mlp_w_down, ln_mlp_g, ln_mlp_b):
    batch, seq, _ = x.shape
    t = batch * seq
    tm = _tile(t, 512)
    tq = _tile(seq, 256)
    rows = _tile(seq, 512)

    h = x.reshape(t, D_MODEL)
    pos = positions.reshape(t, 1).astype(F32)
    inv = ROPE_THETA ** (-jnp.arange(0, ROT_DIM, 2, dtype=F32) / ROT_DIM)
    inv = jnp.concatenate([inv, inv, jnp.zeros((A_HEAD_DIM - ROT_DIM,), F32)])
    inv = jnp.tile(inv, LANES // A_HEAD_DIM).reshape(1, LANES)
    row = lambda p: p.reshape(1, -1)

    for layer in range(DEPTH):
        j = layer // 2
        if layer % 2 == 0:
            w = a_w_in[j].astype(BF16)
            o1, o2, o3, o4 = A_Q, A_Q + A_KV, A_Q + 2 * A_KV, A_Q + 2 * A_KV + A_IQ
            o5 = o4 + IDX_DIM
            wik = jnp.pad(w[:, o4:o5], ((0, 0), (0, LANES - IDX_DIM)))
            wiwt = jnp.pad(w[:, o5:].T, ((0, 16 - IDX_HEADS), (0, 0)))
            q, k, vt, iq, ik, iwt = _dsa_proj(h, pos, inv, w[:, :o1], w[:, o1:o2], w[:, o2:o3].T,
                                              w[:, o3:o4], wik, wiwt, tm=tm, tkv=tq)
            y = _dsa_attn(q, k, vt, iq, ik, iwt, batch=batch, seq=seq, tq=tq)
            w_o = a_w_o[j].astype(BF16)
        else:
            w = b_w_in[j].astype(BF16)
            o1, o2, o3, o4 = B_QK, 2 * B_QK, 2 * B_QK + B_V, 2 * B_QK + 2 * B_V
            wa = jnp.pad(w[:, o4:], ((0, 0), (0, LANES - B_GATE_RANK)))
            q, k, v, r, a = _gla_proj(h, w[:, :o1], w[:, o1:o2], w[:, o2:o3], w[:, o3:o4], wa, tm=tm)
            y = _gla(q, k, v, r, a, b_w_a2[j], row(b_b_a[j]), row(b_g_norm[j]),
                     batch=batch, seq=seq, rows=rows, heads=GLA_HEADS_PER_STEP)
            w_o = b_w_o[j].astype(BF16)
        h = _mix_mlp(y, w_o, h, row(ln_mix_g[layer]), row(ln_mix_b[layer]),
                     mlp_w_up[layer].astype(BF16), mlp_w_down[layer].astype(BF16),
                     row(ln_mlp_g[layer]), row(ln_mlp_b[layer]), tm=tm, tf=1024)
    return h.reshape(batch, seq, D_MODEL)


_forward_jit = jax.jit(_forward)


def kernel(x, positions, a_w_in, a_w_o, b_w_in, b_w_a2, b_b_a, b_g_norm, b_w_o,
           ln_mix_g, ln_mix_b, mlp_w_up, mlp_w_down, ln_mlp_g, ln_mlp_b):
    return _forward_jit(x, positions, a_w_in, a_w_o, b_w_in, b_w_a2, b_b_a, b_g_norm, b_w_o,
                        ln_mix_g, ln_mix_b, mlp_w_up, mlp_w_down, ln_mlp_g, ln_mlp_b)
```

```python
import functools

import jax
import jax.numpy as jnp
from jax import lax
from jax.experimental import pallas as pl
from jax.experimental.pallas import tpu as pltpu

D_MODEL = 1024
DEPTH = 4

A_HEADS = 16
A_KV_HEADS = 4
A_HEAD_DIM = 64
A_REP = A_HEADS // A_KV_HEADS
IDX_HEADS = 8
IDX_DIM = 64
IDX_TOPK_MAX = 256
A_Q = A_HEADS * A_HEAD_DIM
A_KV = A_KV_HEADS * A_HEAD_DIM
A_IQ = IDX_HEADS * IDX_DIM

B_HEADS = 4
B_DK = D_MODEL // 2 // B_HEADS
B_DV = D_MODEL // B_HEADS
B_GATE_RANK = 16
B_GATE_TAU = 16.0
B_CHUNK = 64
B_QK = B_HEADS * B_DK
B_V = B_HEADS * B_DV

D_FF = 4 * D_MODEL

ROPE_THETA = 500000.0
ROT_DIM = A_HEAD_DIM // 4
ROT_HALF = ROT_DIM // 2
LN_EPS = 1e-5
RMS_EPS = 1e-6
DN_ALPHA = (2 * DEPTH) ** 0.25

LANES = 128
VMEM_LIMIT_BYTES = 56 * 1024 * 1024
INT32_MIN = -(2 ** 31)
INT32_MAX = 2 ** 31 - 1
NEG_BIG = -(2.0 ** 100)
EXP_CLAMP = 80.0
LOG2_E = 1.4426950408889634
PV_ROWS = A_HEAD_DIM + 16
SEARCH_VALUE_ROUNDS = 24
SEARCH_ROUNDS_PER_CHECK = 3
FLASH_HEADS = 4
GLA_HEADS_PER_STEP = 4
FOLD_ROWS = 32

BF16 = jnp.bfloat16
F32 = jnp.float32

_NT = (((1,), (1,)), ((), ()))
_TN = (((0,), (0,)), ((), ()))


def _params(*sem):
    return pltpu.CompilerParams(dimension_semantics=sem,
                                vmem_limit_bytes=VMEM_LIMIT_BYTES)


def _const_spec(shape):
    nd = len(shape)
    return pl.BlockSpec(shape, lambda *_: (0,) * nd)


def _float_to_key(x):
    bits = lax.bitcast_convert_type(x, jnp.int32)
    return bits ^ ((bits >> 31) & INT32_MAX)


def _key_to_float(key):
    return lax.bitcast_convert_type(key ^ ((key >> 31) & INT32_MAX), F32)


def _layer_norm(z, g, b):
    mu = jnp.mean(z, axis=-1, keepdims=True)
    zc = z - mu
    var = jnp.mean(zc * zc, axis=-1, keepdims=True)
    return zc * lax.rsqrt(var + LN_EPS) * g + b


def _rot_tables(pos, inv):
    ang = pos * inv
    c = jnp.cos(ang)
    s = jnp.sin(ang)
    lane = lax.broadcasted_iota(jnp.int32, ang.shape, 1) % A_HEAD_DIM
    s_lo = jnp.where(lane < ROT_HALF, -s, 0.0)
    s_hi = jnp.where((lane >= ROT_HALF) & (lane < ROT_DIM), s, 0.0)
    return c, s_lo, s_hi


def _rotate(t, c, s_lo, s_hi):
    w = t.shape[1]
    reps = w // LANES
    if reps > 1:
        c = jnp.concatenate([c] * reps, axis=1)
        s_lo = jnp.concatenate([s_lo] * reps, axis=1)
        s_hi = jnp.concatenate([s_hi] * reps, axis=1)
    up = pltpu.roll(t, w - ROT_HALF, axis=1)
    dn = pltpu.roll(t, ROT_HALF, axis=1)
    return t * c + up * s_lo + dn * s_hi


def _dsa_proj_kernel(h_ref, pos_ref, inv_ref, wq_ref, wk_ref, wvt_ref, wiq_ref, wik_ref, wiwt_ref,
                     q_ref, k_ref, vt_ref, iq_ref, ik_ref, iwt_ref, *, tkv):
    xb = h_ref[...].astype(BF16)
    tm = xb.shape[0]
    c, s_lo, s_hi = _rot_tables(pos_ref[...], inv_ref[...])

    q = jnp.dot(xb, wq_ref[...], preferred_element_type=F32)
    q = _rotate(q, c, s_lo, s_hi) * (A_HEAD_DIM ** -0.5 * LOG2_E)
    for hh in range(A_HEADS):
        q_ref[hh] = q[:, hh * A_HEAD_DIM:(hh + 1) * A_HEAD_DIM].astype(BF16)

    k = jnp.dot(xb, wk_ref[...], preferred_element_type=F32)
    k = _rotate(k, c, s_lo, s_hi)
    vt = lax.dot_general(wvt_ref[...], xb, _NT, preferred_element_type=F32)
    ones_row = jnp.where(lax.broadcasted_iota(jnp.int32, (PV_ROWS - A_HEAD_DIM, tkv), 0) == 0, 1.0, 0.0)
    for g in range(A_KV_HEADS):
        k_ref[g] = k[:, g * A_HEAD_DIM:(g + 1) * A_HEAD_DIM].astype(BF16)
        for j in range(tm // tkv):
            vt_gj = vt[g * A_HEAD_DIM:(g + 1) * A_HEAD_DIM, j * tkv:(j + 1) * tkv]
            vt_ref[g, j] = jnp.concatenate([vt_gj, ones_row], axis=0).astype(BF16)

    iq = jnp.dot(xb, wiq_ref[...], preferred_element_type=F32)
    iq = _rotate(iq, c, s_lo, s_hi) * (IDX_DIM ** -0.5)
    for hh in range(IDX_HEADS):
        iq_ref[hh] = iq[:, hh * IDX_DIM:(hh + 1) * IDX_DIM].astype(BF16)

    x = jnp.dot(xb, wik_ref[...], preferred_element_type=F32)
    first = lax.broadcasted_iota(jnp.int32, x.shape, 1) < IDX_DIM
    x = _rotate(x, jnp.where(first, c, 1.0), jnp.where(first, s_lo, 0.0), jnp.where(first, s_hi, 0.0))
    ik_ref[...] = x[:, :IDX_DIM].astype(BF16)
    iwt = lax.dot_general(wiwt_ref[...], xb, _NT, preferred_element_type=F32)
    iwt_ref[...] = iwt[:IDX_HEADS] * (IDX_HEADS ** -0.5)


def _dsa_proj(h, pos, inv, wq, wk, wvt, wiq, wik, wiwt, *, tm, tkv):
    t = h.shape[0]
    row = lambda i: (i, 0)
    head = lambda i: (0, i, 0)
    return pl.pallas_call(
        functools.partial(_dsa_proj_kernel, tkv=tkv),
        grid=(t // tm,),
        in_specs=[pl.BlockSpec((tm, D_MODEL), row), pl.BlockSpec((tm, 1), row),
                  _const_spec(inv.shape), _const_spec(wq.shape), _const_spec(wk.shape),
                  _const_spec(wvt.shape), _const_spec(wiq.shape), _const_spec(wik.shape),
                  _const_spec(wiwt.shape)],
        out_specs=[pl.BlockSpec((A_HEADS, tm, A_HEAD_DIM), head),
                   pl.BlockSpec((A_KV_HEADS, tm, A_HEAD_DIM), head),
                   pl.BlockSpec((A_KV_HEADS, tm // tkv, PV_ROWS, tkv), lambda i: (0, i, 0, 0)),
                   pl.BlockSpec((IDX_HEADS, tm, IDX_DIM), head),
                   pl.BlockSpec((tm, IDX_DIM), row),
                   pl.BlockSpec((IDX_HEADS, tm), lambda i: (0, i))],
        out_shape=[jax.ShapeDtypeStruct((A_HEADS, t, A_HEAD_DIM), BF16),
                   jax.ShapeDtypeStruct((A_KV_HEADS, t, A_HEAD_DIM), BF16),
                   jax.ShapeDtypeStruct((A_KV_HEADS, t // tkv, PV_ROWS, tkv), BF16),
                   jax.ShapeDtypeStruct((IDX_HEADS, t, IDX_DIM), BF16),
                   jax.ShapeDtypeStruct((t, IDX_DIM), BF16),
                   jax.ShapeDtypeStruct((IDX_HEADS, t), F32)],
        compiler_params=_params("parallel"),
        name="dsa_proj",
    )(h, pos, inv, wq, wk, wvt, wiq, wik, wiwt)


def _dsa_attn_kernel(q_ref, iq_ref, iwt_ref, k_ref, vt_ref, ik_ref, o_ref,
                     slab, m_sc, acc_sc, cut_sc, ot_sc, *, tq, n_qt, top_k):
    i = pl.program_id(1)
    n_kt = i + 1
    q_pos = i * tq + lax.broadcasted_iota(jnp.int32, (1, tq), 1)
    key_off = lax.broadcasted_iota(jnp.int32, (tq, tq), 0)

    iw = iwt_ref[...]

    def fold(x, op):
        return op(x.reshape(-1, FOLD_ROWS, tq), axis=0)

    n_pairs = (n_kt + 1) // 2

    def score_pair(j, carry):
        mx, mn, c0, c1 = carry
        for kt in (2 * j, 2 * j + 1):
            ik_t = ik_ref[pl.ds(pl.multiple_of(jnp.minimum(kt, n_qt - 1) * tq, tq), tq), :]
            sc = jnp.zeros((tq, tq), F32)
            for hh in range(IDX_HEADS):
                s = lax.dot_general(ik_t, iq_ref[hh], _NT, preferred_element_type=F32)
                sc = sc + iw[hh:hh + 1, :] * jnp.maximum(s, 0.0)
            key = _float_to_key(sc)
            causal = (kt * tq + key_off) <= q_pos
            key = jnp.where(causal, key, INT32_MIN)
            slab[kt] = key
            mx = jnp.maximum(mx, fold(key, jnp.max))
            mn = jnp.minimum(mn, fold(jnp.where(causal, key, INT32_MAX), jnp.min))
            c0 = c0 + fold(jnp.where(key >= 0, 1, 0), jnp.sum)
            c1 = c1 + fold(jnp.where(key >= 1, 1, 0), jnp.sum)
        return mx, mn, c0, c1

    part = (FOLD_ROWS, tq)
    init = (jnp.full(part, INT32_MIN, jnp.int32), jnp.full(part, INT32_MAX, jnp.int32),
            jnp.zeros(part, jnp.int32), jnp.zeros(part, jnp.int32))
    mx, mn, c0, c1 = lax.fori_loop(0, n_pairs, score_pair, init)
    key_max = jnp.max(mx, axis=0, keepdims=True)
    key_min = jnp.min(mn, axis=0, keepdims=True)
    cnt_nonneg = jnp.sum(c0, axis=0, keepdims=True)
    cnt_pos = jnp.sum(c1, axis=0, keepdims=True)

    k_row = jnp.minimum(top_k, q_pos + 1)

    def count(pred):
        def body(j, acc):
            hit = jnp.where(pred(slab[pl.ds(2 * j, 2)]), 1, 0)
            return acc + fold(hit, jnp.sum)
        acc = lax.fori_loop(0, n_pairs, body, jnp.zeros(part, jnp.int32))
        return jnp.sum(acc, axis=0, keepdims=True)

    pos_case = cnt_pos >= k_row
    zero_case = jnp.logical_not(pos_case) & (cnt_nonneg >= k_row)
    lo0 = jnp.where(pos_case, 1, jnp.where(zero_case, 0, key_min))
    clo0 = jnp.where(pos_case, cnt_pos, jnp.where(zero_case, cnt_nonneg, q_pos + 1))
    hi0 = jnp.where(pos_case, key_max + 1, jnp.where(zero_case, 1, 0))
    chi0 = jnp.where(pos_case, 0, jnp.where(zero_case, cnt_pos, cnt_nonneg))

    def n_active(lo, hi, clo):
        return jnp.max(jnp.where((clo != k_row) & (hi - 1 > lo), 1, 0))

    def search_cond(carry):
        it, nact = carry[0], carry[1]
        return (nact > 0) & (it < SEARCH_VALUE_ROUNDS + 33)

    def search_round(it, lo, hi, clo, chi, bisect):
        active = (clo != k_row) & (hi - 1 > lo)
        lov = _key_to_float(lo)
        hiv = _key_to_float(hi)
        log_clo = jnp.log(clo.astype(F32) + 0.5)
        frac_interp = (log_clo - log_k) / (log_clo - jnp.log(chi.astype(F32) + 0.5))
        frac = jnp.where(bisect, 0.5, frac_interp)
        cand_value = _float_to_key(lov + (hiv - lov) * frac)
        cand_key = lo + ((hi - lo) >> 1)
        cand = jnp.where(it >= SEARCH_VALUE_ROUNDS, cand_key, cand_value)
        cand = jnp.minimum(jnp.maximum(cand, lo + 1), hi - 1)
        cand = jnp.where(active, cand, lo)
        cnt = count(lambda ky: ky >= cand)
        ok = cnt >= k_row
        return (jnp.where(ok, cand, lo), jnp.where(ok, hi, cand),
                jnp.where(ok, cnt, clo), jnp.where(ok, chi, cnt))

    log_k = jnp.log(k_row.astype(F32) + 0.5)

    def search_body(carry):
        it, _, lo, hi, clo, chi = carry
        for r in range(SEARCH_ROUNDS_PER_CHECK):
            bisect = (it != 0) if r == SEARCH_ROUNDS_PER_CHECK - 1 else (it == 0)
            lo, hi, clo, chi = search_round(it, lo, hi, clo, chi, bisect)
        return it + SEARCH_ROUNDS_PER_CHECK, n_active(lo, hi, clo), lo, hi, clo, chi

    _, _, thr, _, nge, _ = lax.while_loop(
        search_cond, search_body, (jnp.int32(0), n_active(lo0, hi0, clo0), lo0, hi0, clo0, chi0))

    cut_sc[...] = jnp.full((1, tq), n_kt * tq, jnp.int32)
    has_ties = jnp.max(nge - k_row) > 0

    @pl.when(has_ties)
    def _():
        n_gt = count(lambda ky: ky > thr)
        need = (k_row - n_gt).astype(F32)
        tri = jnp.where(lax.broadcasted_iota(jnp.int32, (tq, tq), 1) <= key_off, 1.0, 0.0).astype(BF16)

        def tie_tile(kt, carry):
            seen, best = carry
            eq = slab[kt] == thr
            rank = jnp.dot(tri, jnp.where(eq, 1.0, 0.0).astype(BF16),
                           preferred_element_type=F32) + seen
            hit = eq & (rank == need)
            best = jnp.maximum(best, fold(jnp.where(hit, kt * tq + key_off, -1), jnp.max))
            return rank[tq - 1:tq, :], best

        _, best = lax.fori_loop(0, n_kt, tie_tile,
                                (jnp.zeros((1, tq), F32), jnp.full(part, -1, jnp.int32)))
        cut_sc[...] = jnp.max(best, axis=0, keepdims=True)

    cut = cut_sc[...]

    m_sc[...] = jnp.full(m_sc.shape, NEG_BIG, F32)
    acc_sc[...] = jnp.zeros(acc_sc.shape, F32)

    n_steps = A_HEADS // FLASH_HEADS

    def mask_bias(kt):
        keys = slab[kt]
        sel = (keys > thr) | ((keys == thr) & ((kt * tq + key_off) <= cut))
        bias = jnp.where(sel, 0.0, NEG_BIG).astype(BF16)
        return jnp.concatenate([bias] * FLASH_HEADS, axis=1)

    def logits(kt, st, bias):
        qs = q_ref[st * FLASH_HEADS:(st + 1) * FLASH_HEADS].reshape(FLASH_HEADS * tq, A_HEAD_DIM)
        kg = k_ref[st * FLASH_HEADS // A_REP, pl.ds(pl.multiple_of(kt * tq, tq), tq), :]
        return lax.dot_general(kg, qs, _NT, preferred_element_type=F32).astype(BF16) + bias

    def attn_tile(kt, carry):
        s_next, bias = carry
        for st in range(n_steps):
            s = s_next
            if st + 1 < n_steps:
                s_next = logits(kt, st + 1, bias)
            else:
                kt_next = jnp.minimum(kt + 1, n_kt - 1)
                bias = mask_bias(kt_next)
                s_next = logits(kt_next, 0, bias)
            m_prev = m_sc[st]
            m_new = jnp.maximum(m_prev, jnp.max(s, axis=0, keepdims=True).astype(F32))
            alpha = jnp.exp2(m_prev - m_new)
            p = jnp.exp2(s - m_new.astype(BF16))
            acc_sc[st] = alpha * acc_sc[st] + jnp.dot(vt_ref[st * FLASH_HEADS // A_REP, kt], p,
                                                      preferred_element_type=F32)
            m_sc[st] = m_new
        return s_next, bias

    bias0 = mask_bias(0)
    lax.fori_loop(0, n_kt, attn_tile, (logits(0, 0, bias0), bias0))

    for st in range(n_steps):
        acc = acc_sc[st]
        og = acc[:A_HEAD_DIM] / acc[A_HEAD_DIM:A_HEAD_DIM + 1]
        for r in range(FLASH_HEADS):
            hh = st * FLASH_HEADS + r
            ot_sc[hh * A_HEAD_DIM:(hh + 1) * A_HEAD_DIM, :] = og[:, r * tq:(r + 1) * tq]
    o_ref[...] = ot_sc[...].T.astype(BF16)


def _dsa_attn(q, k, vt, iq, ik, iwt, *, batch, seq, tq):
    t = batch * seq
    nq = seq // tq
    top_k = min(IDX_TOPK_MAX, seq // 4)
    qtile = lambda b, i: (0, b * nq + i, 0)
    whole = lambda b, i: (0, b, 0)
    once = pl.Buffered(1)
    return pl.pallas_call(
        functools.partial(_dsa_attn_kernel, tq=tq, n_qt=nq, top_k=top_k),
        grid=(batch, nq),
        in_specs=[pl.BlockSpec((A_HEADS, tq, A_HEAD_DIM), qtile),
                  pl.BlockSpec((IDX_HEADS, tq, IDX_DIM), qtile),
                  pl.BlockSpec((IDX_HEADS, tq), lambda b, i: (0, b * nq + i)),
                  pl.BlockSpec((A_KV_HEADS, seq, A_HEAD_DIM), whole, pipeline_mode=once),
                  pl.BlockSpec((A_KV_HEADS, nq, PV_ROWS, tq), lambda b, i: (0, b, 0, 0),
                               pipeline_mode=once),
                  pl.BlockSpec((seq, IDX_DIM), lambda b, i: (b, 0), pipeline_mode=once)],
        out_specs=pl.BlockSpec((tq, A_Q), lambda b, i: (b * nq + i, 0)),
        out_shape=jax.ShapeDtypeStruct((t, A_Q), BF16),
        scratch_shapes=[pltpu.VMEM((nq + nq % 2, tq, tq), jnp.int32),
                        pltpu.VMEM((A_HEADS // FLASH_HEADS, 1, FLASH_HEADS * tq), F32),
                        pltpu.VMEM((A_HEADS // FLASH_HEADS, PV_ROWS, FLASH_HEADS * tq), F32),
                        pltpu.VMEM((1, tq), jnp.int32),
                        pltpu.VMEM((A_Q, tq), F32)],
        compiler_params=_params("parallel", "arbitrary"),
        name="dsa_attn",
    )(q, iq, iwt, k, vt, ik)


def _mix_mlp_kernel(y_ref, wo_ref, h_ref, g1_ref, b1_ref, wu_ref, wd_ref, g2_ref, b2_ref, o_ref, *, tf):
    mix = jnp.dot(y_ref[...], wo_ref[...], preferred_element_type=F32)
    h = _layer_norm(DN_ALPHA * h_ref[...] + mix, g1_ref[...], b1_ref[...])
    xb = h.astype(BF16)
    acc = jnp.zeros(h.shape, F32)
    for c in range(D_FF // tf):
        u = jnp.dot(xb, wu_ref[:, c * tf:(c + 1) * tf], preferred_element_type=F32)
        u = jnp.square(jnp.maximum(u, 0.0)).astype(BF16)
        acc = acc + jnp.dot(u, wd_ref[c * tf:(c + 1) * tf, :], preferred_element_type=F32)
    o_ref[...] = _layer_norm(DN_ALPHA * h + acc, g2_ref[...], b2_ref[...])


def _mix_mlp(y, wo, h, g1, b1, wu, wd, g2, b2, *, tm, tf):
    t, kdim = y.shape
    row = lambda i: (i, 0)
    once = pl.Buffered(1)
    resident = lambda w: pl.BlockSpec(w.shape, lambda i: (0, 0), pipeline_mode=once)
    return pl.pallas_call(
        functools.partial(_mix_mlp_kernel, tf=tf),
        grid=(t // tm,),
        in_specs=[pl.BlockSpec((tm, kdim), row), resident(wo), pl.BlockSpec((tm, D_MODEL), row),
                  _const_spec(g1.shape), _const_spec(b1.shape), resident(wu), resident(wd),
                  _const_spec(g2.shape), _const_spec(b2.shape)],
        out_specs=pl.BlockSpec((tm, D_MODEL), row),
        out_shape=jax.ShapeDtypeStruct((t, D_MODEL), F32),
        compiler_params=_params("parallel"),
        name="mix_mlp",
    )(y, wo, h, g1, b1, wu, wd, g2, b2)


def _gla_proj_kernel(h_ref, wq_ref, wk_ref, wv_ref, wr_ref, wa_ref,
                     q_ref, k_ref, v_ref, r_ref, a_ref):
    xb = h_ref[...].astype(BF16)
    q = jnp.dot(xb, wq_ref[...], preferred_element_type=F32)
    q_ref[...] = (q * (B_DK ** -0.5)).astype(BF16)
    k_ref[...] = jnp.dot(xb, wk_ref[...], preferred_element_type=F32).astype(BF16)
    v_ref[...] = jnp.dot(xb, wv_ref[...], preferred_element_type=F32).astype(BF16)
    r_ref[...] = jnp.dot(xb, wr_ref[...], preferred_element_type=F32).astype(BF16)
    a = jnp.dot(xb, wa_ref[...], preferred_element_type=F32)
    a_ref[...] = a[:, :B_GATE_RANK]


def _gla_proj(h, wq, wk, wv, wr, wa, *, tm):
    t = h.shape[0]
    row = lambda i: (i, 0)
    return pl.pallas_call(
        _gla_proj_kernel,
        grid=(t // tm,),
        in_specs=[pl.BlockSpec((tm, D_MODEL), row), _const_spec(wq.shape), _const_spec(wk.shape),
                  _const_spec(wv.shape), _const_spec(wr.shape), _const_spec(wa.shape)],
        out_specs=[pl.BlockSpec((tm, B_QK), row), pl.BlockSpec((tm, B_QK), row),
                   pl.BlockSpec((tm, B_V), row), pl.BlockSpec((tm, B_V), row),
                   pl.BlockSpec((tm, B_GATE_RANK), row)],
        out_shape=[jax.ShapeDtypeStruct((t, B_QK), BF16), jax.ShapeDtypeStruct((t, B_QK), BF16),
                   jax.ShapeDtypeStruct((t, B_V), BF16), jax.ShapeDtypeStruct((t, B_V), BF16),
                   jax.ShapeDtypeStruct((t, B_GATE_RANK), F32)],
        compiler_params=_params("parallel"),
        name="gla_proj",
    )(h, wq, wk, wv, wr, wa)


def _gla_kernel(q_ref, k_ref, v_ref, r_ref, a_ref, w2_ref, ba_ref, gn_ref, y_ref, st_ref, *, rows, heads):
    n_chunks = rows // B_CHUNK

    @pl.when(pl.program_id(2) == 0)
    def _():
        st_ref[...] = jnp.zeros(st_ref.shape, F32)

    x = jnp.dot(a_ref[...], w2_ref[...], preferred_element_type=F32,
                precision=lax.Precision.HIGHEST) + ba_ref[...]
    log_a = (jnp.minimum(x, 0.0) - jnp.log1p(jnp.exp(-jnp.abs(x)))) / B_GATE_TAU

    row_in_chunk = lax.broadcasted_iota(jnp.int32, log_a.shape, 0) % B_CHUNK
    b = log_a
    step = 1
    while step < B_CHUNK:
        b = b + jnp.where(row_in_chunk >= step, pltpu.roll(b, step, axis=0), 0.0)
        step *= 2

    tril = (lax.broadcasted_iota(jnp.int32, (B_CHUNK, B_CHUNK), 1)
            <= lax.broadcasted_iota(jnp.int32, (B_CHUNK, B_CHUNK), 0))
    gn = gn_ref[...]

    prep = []
    for hd in range(heads):
        ks = slice(hd * B_DK, (hd + 1) * B_DK)
        b3 = b[:, ks].reshape(n_chunks, B_CHUNK, B_DK)
        b_last = b3[:, B_CHUNK - 1:B_CHUNK, :]
        b_mid = b3[:, B_CHUNK // 2 - 1:B_CHUNK // 2, :]
        q3 = q_ref[:, ks].astype(F32).reshape(n_chunks, B_CHUNK, B_DK)
        k3 = k_ref[:, ks].astype(F32).reshape(n_chunks, B_CHUNK, B_DK)
        prep.append(dict(
            q_in=(q3 * jnp.exp(b3)).astype(BF16),
            k_out=(k3 * jnp.exp(b_last - b3)).astype(BF16),
            q_mid=(q3 * jnp.exp(jnp.minimum(b3 - b_mid, EXP_CLAMP))).astype(BF16),
            k_mid=(k3 * jnp.exp(jnp.minimum(b_mid - b3, EXP_CLAMP))).astype(BF16),
            decay=jnp.exp(b_last),
            state=st_ref[hd]))

    for n in range(n_chunks):
        lo, hi = n * B_CHUNK, (n + 1) * B_CHUNK
        for hd in range(heads):
            p = prep[hd]
            vs = slice(hd * B_DV, (hd + 1) * B_DV)
            vn = v_ref[lo:hi, vs]
            attn = lax.dot_general(p["q_mid"][n], p["k_mid"][n], _NT, preferred_element_type=F32)
            attn = jnp.where(tril, attn, 0.0).astype(BF16)
            o = jnp.dot(attn, vn, preferred_element_type=F32)
            o = o + lax.dot_general(p["q_in"][n], p["state"].astype(BF16), _NT,
                                    preferred_element_type=F32)
            upd = lax.dot_general(vn, p["k_out"][n], _TN, preferred_element_type=F32)
            p["state"] = p["decay"][n] * p["state"] + upd
            o = o * lax.rsqrt(jnp.mean(o * o, axis=-1, keepdims=True) + RMS_EPS) * gn
            rn = r_ref[lo:hi, vs].astype(F32)
            gate = rn / (1.0 + jnp.exp(-rn))
            y_ref[lo:hi, vs] = (o * gate).astype(BF16)
    for hd in range(heads):
        st_ref[hd] = prep[hd]["state"]


def _gla(q, k, v, r, a, w2, ba, gn, *, batch, seq, rows, heads):
    t = batch * seq
    nb = seq // rows
    tile = lambda b, h, j: (b * nb + j, h)
    return pl.pallas_call(
        functools.partial(_gla_kernel, rows=rows, heads=heads),
        grid=(batch, B_HEADS // heads, nb),
        in_specs=[pl.BlockSpec((rows, heads * B_DK), tile), pl.BlockSpec((rows, heads * B_DK), tile),
                  pl.BlockSpec((rows, heads * B_DV), tile), pl.BlockSpec((rows, heads * B_DV), tile),
                  pl.BlockSpec((rows, B_GATE_RANK), lambda b, h, j: (b * nb + j, 0)),
                  pl.BlockSpec((B_GATE_RANK, heads * B_DK), lambda b, h, j: (0, h)),
                  pl.BlockSpec((1, heads * B_DK), lambda b, h, j: (0, h)),
                  _const_spec(gn.shape)],
        out_specs=pl.BlockSpec((rows, heads * B_DV), tile),
        out_shape=jax.ShapeDtypeStruct((t, B_V), BF16),
        scratch_shapes=[pltpu.VMEM((heads, B_DV, B_DK), F32)],
        compiler_params=_params("parallel", "parallel", "arbitrary"),
        name="gla",
    )(q, k, v, r, a, w2, ba, gn)


def _tile(n, pref):
    return pref if n % pref == 0 else n


def _forward(x, positions, a_w_in, a_w_o, b_w_in, b_w_a2, b_b_a, b_g_norm, b_w_o,
             ln_mix_g, ln_mix_b, mlp_w_up, mlp_w_down, ln_mlp_g, ln_mlp_b):
    batch, seq, _ = x.shape
    t = batch * seq
    tm = _tile(t, 512)
    tq = _tile(seq, 256)
    rows = _tile(seq, 512)

    h = x.reshape(t, D_MODEL)
    pos = positions.reshape(t, 1).astype(F32)
    inv = ROPE_THETA ** (-jnp.arange(0, ROT_DIM, 2, dtype=F32) / ROT_DIM)
    inv = jnp.concatenate([inv, inv, jnp.zeros((A_HEAD_DIM - ROT_DIM,), F32)])
    inv = jnp.tile(inv, LANES // A_HEAD_DIM).reshape(1, LANES)
    row = lambda p: p.reshape(1, -1)

    for layer in range(DEPTH):
        j = layer // 2
        if layer % 2 == 0:
            w = a_w_in[j].astype(BF16)
            o1, o2, o3, o4 = A_Q, A_Q + A_KV, A_Q + 2 * A_KV, A_Q + 2 * A_KV + A_IQ
            o5 = o4 + IDX_DIM
            wik = jnp.pad(w[:, o4:o5], ((0, 0), (0, LANES - IDX_DIM)))
            wiwt = jnp.pad(w[:, o5:].T, ((0, 16 - IDX_HEADS), (0, 0)))
            q, k, vt, iq, ik, iwt = _dsa_proj(h, pos, inv, w[:, :o1], w[:, o1:o2], w[:, o2:o3].T,
                                              w[:, o3:o4], wik, wiwt, tm=tm, tkv=tq)
            y = _dsa_attn(q, k, vt, iq, ik, iwt, batch=batch, seq=seq, tq=tq)
            w_o = a_w_o[j].astype(BF16)
        else:
            w = b_w_in[j].astype(BF16)
            o1, o2, o3, o4 = B_QK, 2 * B_QK, 2 * B_QK + B_V, 2 * B_QK + 2 * B_V
            wa = jnp.pad(w[:, o4:], ((0, 0), (0, LANES - B_GATE_RANK)))
            q, k, v, r, a = _gla_proj(h, w[:, :o1], w[:, o1:o2], w[:, o2:o3], w[:, o3:o4], wa, tm=tm)
            y = _gla(q, k, v, r, a, b_w_a2[j], row(b_b_a[j]), row(b_g_norm[j]),
                     batch=batch, seq=seq, rows=rows, heads=GLA_HEADS_PER_STEP)
            w_o = b_w_o[j].astype(BF16)
        h = _mix_mlp(y, w_o, h, row(ln_mix_g[layer]), row(ln_mix_b[layer]),
                     mlp_w_up[layer].astype(BF16), mlp_w_down[layer].astype(BF16),
                     row(ln_mlp_g[layer]), row(ln_mlp_b[layer]), tm=tm, tf=1024)
    return h.reshape(batch, seq, D_MODEL)


_forward_jit = jax.jit(_forward)


def kernel(x, positions, a_w_in, a_w_o, b_w_in, b_w_a2, b_b_a, b_g_norm, b_w_o,
           ln_mix_g, ln_mix_b, mlp_w_up, mlp_w_down, ln_mlp_g, ln_mlp_b):
    return _forward_jit(x, positions, a_w_in, a_w_o, b_w_in, b_w_a2, b_b_a, b_g_norm, b_w_o,
                        ln_mix_g, ln_mix_b, mlp_w_up, mlp_w_down, ln_mlp_g, ln_mlp_b)
```

```python
import functools

import jax
import jax.numpy as jnp
from jax import lax
from jax.experimental import pallas as pl
from jax.experimental.pallas import tpu as pltpu

D_MODEL = 1024
DEPTH = 4

A_HEADS = 16
A_KV_HEADS = 4
A_HEAD_DIM = 64
A_REP = A_HEADS // A_KV_HEADS
IDX_HEADS = 8
IDX_DIM = 64
IDX_TOPK_MAX = 256
A_Q = A_HEADS * A_HEAD_DIM
A_KV = A_KV_HEADS * A_HEAD_DIM
A_IQ = IDX_HEADS * IDX_DIM

B_HEADS = 4
B_DK = D_MODEL // 2 // B_HEADS
B_DV = D_MODEL // B_HEADS
B_GATE_RANK = 16
B_GATE_TAU = 16.0
B_CHUNK = 64
B_QK = B_HEADS * B_DK
B_V = B_HEADS * B_DV

D_FF = 4 * D_MODEL

ROPE_THETA = 500000.0
ROT_DIM = A_HEAD_DIM // 4
ROT_HALF = ROT_DIM // 2
LN_EPS = 1e-5
RMS_EPS = 1e-6
DN_ALPHA = (2 * DEPTH) ** 0.25

LANES = 128
VMEM_LIMIT_BYTES = 56 * 1024 * 1024
INT32_MIN = -(2 ** 31)
INT32_MAX = 2 ** 31 - 1
ZERO_BAND_MIN = -(2 ** 23)
NEG_BIG = -(2.0 ** 100)
EXP_CLAMP = 80.0
LOG2_E = 1.4426950408889634
PV_ROWS = A_HEAD_DIM + 16
SEARCH_VALUE_ROUNDS = 24
SEARCH_ROUNDS_PER_CHECK = 3
FLASH_HEADS = 4
GLA_HEADS_PER_STEP = 4
FOLD_ROWS = 32

BF16 = jnp.bfloat16
F32 = jnp.float32

_NT = (((1,), (1,)), ((), ()))
_TN = (((0,), (0,)), ((), ()))


def _params(*sem):
    return pltpu.CompilerParams(dimension_semantics=sem,
                                vmem_limit_bytes=VMEM_LIMIT_BYTES)


def _const_spec(shape):
    nd = len(shape)
    return pl.BlockSpec(shape, lambda *_: (0,) * nd)


def _float_to_key(x):
    bits = lax.bitcast_convert_type(x, jnp.int32)
    return bits ^ ((bits >> 31) & INT32_MAX)


def _key_to_float(key):
    return lax.bitcast_convert_type(key ^ ((key >> 31) & INT32_MAX), F32)


def _layer_norm(z, g, b):
    mu = jnp.mean(z, axis=-1, keepdims=True)
    zc = z - mu
    var = jnp.mean(zc * zc, axis=-1, keepdims=True)
    return zc * lax.rsqrt(var + LN_EPS) * g + b


def _rot_tables(pos, inv):
    ang = pos * inv
    c = jnp.cos(ang)
    s = jnp.sin(ang)
    lane = lax.broadcasted_iota(jnp.int32, ang.shape, 1) % A_HEAD_DIM
    s_lo = jnp.where(lane < ROT_HALF, -s, 0.0)
    s_hi = jnp.where((lane >= ROT_HALF) & (lane < ROT_DIM), s, 0.0)
    return c, s_lo, s_hi


def _rotate(t, c, s_lo, s_hi):
    w = t.shape[1]
    reps = w // LANES
    if reps > 1:
        c = jnp.concatenate([c] * reps, axis=1)
        s_lo = jnp.concatenate([s_lo] * reps, axis=1)
        s_hi = jnp.concatenate([s_hi] * reps, axis=1)
    up = pltpu.roll(t, w - ROT_HALF, axis=1)
    dn = pltpu.roll(t, ROT_HALF, axis=1)
    return t * c + up * s_lo + dn * s_hi


def _dsa_proj_kernel(h_ref, pos_ref, inv_ref, wq_ref, wk_ref, wvt_ref, wiq_ref, wik_ref, wiwt_ref,
                     q_ref, k_ref, vt_ref, iq_ref, ik_ref, iwt_ref, *, tkv):
    xb = h_ref[...].astype(BF16)
    tm = xb.shape[0]
    c, s_lo, s_hi = _rot_tables(pos_ref[...], inv_ref[...])

    q = jnp.dot(xb, wq_ref[...], preferred_element_type=F32)
    q = _rotate(q, c, s_lo, s_hi) * (A_HEAD_DIM ** -0.5 * LOG2_E)
    for hh in range(A_HEADS):
        q_ref[hh] = q[:, hh * A_HEAD_DIM:(hh + 1) * A_HEAD_DIM].astype(BF16)

    k = jnp.dot(xb, wk_ref[...], preferred_element_type=F32)
    k = _rotate(k, c, s_lo, s_hi)
    vt = lax.dot_general(wvt_ref[...], xb, _NT, preferred_element_type=F32)
    ones_row = jnp.where(lax.broadcasted_iota(jnp.int32, (PV_ROWS - A_HEAD_DIM, tkv), 0) == 0, 1.0, 0.0)
    for g in range(A_KV_HEADS):
        k_ref[g] = k[:, g * A_HEAD_DIM:(g + 1) * A_HEAD_DIM].astype(BF16)
        for j in range(tm // tkv):
            vt_gj = vt[g * A_HEAD_DIM:(g + 1) * A_HEAD_DIM, j * tkv:(j + 1) * tkv]
            vt_ref[g, j] = jnp.concatenate([vt_gj, ones_row], axis=0).astype(BF16)

    iq = jnp.dot(xb, wiq_ref[...], preferred_element_type=F32)
    iq = _rotate(iq, c, s_lo, s_hi) * (IDX_DIM ** -0.5)
    for hh in range(IDX_HEADS):
        iq_ref[hh] = iq[:, hh * IDX_DIM:(hh + 1) * IDX_DIM].astype(BF16)

    x = jnp.dot(xb, wik_ref[...], preferred_element_type=F32)
    first = lax.broadcasted_iota(jnp.int32, x.shape, 1) < IDX_DIM
    x = _rotate(x, jnp.where(first, c, 1.0), jnp.where(first, s_lo, 0.0), jnp.where(first, s_hi, 0.0))
    ik_ref[...] = x[:, :IDX_DIM].astype(BF16)
    iwt = lax.dot_general(wiwt_ref[...], xb, _NT, preferred_element_type=F32)
    iwt_ref[...] = iwt[:IDX_HEADS] * (IDX_HEADS ** -0.5)


def _dsa_proj(h, pos, inv, wq, wk, wvt, wiq, wik, wiwt, *, tm, tkv):
    t = h.shape[0]
    row = lambda i: (i, 0)
    head = lambda i: (0, i, 0)
    return pl.pallas_call(
        functools.partial(_dsa_proj_kernel, tkv=tkv),
        grid=(t // tm,),
        in_specs=[pl.BlockSpec((tm, D_MODEL), row), pl.BlockSpec((tm, 1), row),
                  _const_spec(inv.shape), _const_spec(wq.shape), _const_spec(wk.shape),
                  _const_spec(wvt.shape), _const_spec(wiq.shape), _const_spec(wik.shape),
                  _const_spec(wiwt.shape)],
        out_specs=[pl.BlockSpec((A_HEADS, tm, A_HEAD_DIM), head),
                   pl.BlockSpec((A_KV_HEADS, tm, A_HEAD_DIM), head),
                   pl.BlockSpec((A_KV_HEADS, tm // tkv, PV_ROWS, tkv), lambda i: (0, i, 0, 0)),
                   pl.BlockSpec((IDX_HEADS, tm, IDX_DIM), head),
                   pl.BlockSpec((tm, IDX_DIM), row),
                   pl.BlockSpec((IDX_HEADS, tm), lambda i: (0, i))],
        out_shape=[jax.ShapeDtypeStruct((A_HEADS, t, A_HEAD_DIM), BF16),
                   jax.ShapeDtypeStruct((A_KV_HEADS, t, A_HEAD_DIM), BF16),
                   jax.ShapeDtypeStruct((A_KV_HEADS, t // tkv, PV_ROWS, tkv), BF16),
                   jax.ShapeDtypeStruct((IDX_HEADS, t, IDX_DIM), BF16),
                   jax.ShapeDtypeStruct((t, IDX_DIM), BF16),
                   jax.ShapeDtypeStruct((IDX_HEADS, t), F32)],
        compiler_params=_params("parallel"),
        name="dsa_proj",
    )(h, pos, inv, wq, wk, wvt, wiq, wik, wiwt)


def _dsa_attn_kernel(q_ref, iq_ref, iwt_ref, k_ref, vt_ref, ik_ref, o_ref,
                     slab, m_sc, acc_sc, cut_sc, ot_sc, *, tq, n_qt, top_k):
    i = pl.program_id(1)
    n_kt = i + 1
    q_pos = i * tq + lax.broadcasted_iota(jnp.int32, (1, tq), 1)
    key_off = lax.broadcasted_iota(jnp.int32, (tq, tq), 0)

    iw = iwt_ref[...]

    def fold(x, op):
        return op(x.reshape(-1, FOLD_ROWS, tq), axis=0)

    n_pairs = (n_kt + 1) // 2

    def score_pair(j, carry):
        mx, mn, c0, c1 = carry
        for kt in (2 * j, 2 * j + 1):
            ik_t = ik_ref[pl.ds(pl.multiple_of(jnp.minimum(kt, n_qt - 1) * tq, tq), tq), :]
            sc = jnp.zeros((tq, tq), F32)
            for hh in range(IDX_HEADS):
                s = lax.dot_general(ik_t, iq_ref[hh], _NT, preferred_element_type=F32)
                sc = sc + iw[hh:hh + 1, :] * jnp.maximum(s, 0.0)
            key = _float_to_key(sc)
            idx = kt * tq + key_off
            key = jnp.where(key == 0, -1 - idx, key)
            causal = idx <= q_pos
            key = jnp.where(causal, key, INT32_MIN)
            slab[kt] = key
            mx = jnp.maximum(mx, fold(key, jnp.max))
            mn = jnp.minimum(mn, fold(jnp.where(causal, key, INT32_MAX), jnp.min))
            c0 = c0 + fold(jnp.where(key >= ZERO_BAND_MIN, 1, 0), jnp.sum)
            c1 = c1 + fold(jnp.where(key >= 1, 1, 0), jnp.sum)
        return mx, mn, c0, c1

    part = (FOLD_ROWS, tq)
    init = (jnp.full(part, INT32_MIN, jnp.int32), jnp.full(part, INT32_MAX, jnp.int32),
            jnp.zeros(part, jnp.int32), jnp.zeros(part, jnp.int32))
    mx, mn, c0, c1 = lax.fori_loop(0, n_pairs, score_pair, init)
    key_max = jnp.max(mx, axis=0, keepdims=True)
    key_min = jnp.min(mn, axis=0, keepdims=True)
    cnt_nonneg = jnp.sum(c0, axis=0, keepdims=True)
    cnt_pos = jnp.sum(c1, axis=0, keepdims=True)

    k_row = jnp.minimum(top_k, q_pos + 1)

    def count(pred):
        def body(j, acc):
            hit = jnp.where(pred(slab[pl.ds(2 * j, 2)]), 1, 0)
            return acc + fold(hit, jnp.sum)
        acc = lax.fori_loop(0, n_pairs, body, jnp.zeros(part, jnp.int32))
        return jnp.sum(acc, axis=0, keepdims=True)

    pos_case = cnt_pos >= k_row
    zero_case = jnp.logical_not(pos_case) & (cnt_nonneg >= k_row)
    lo0 = jnp.where(pos_case, 1, jnp.where(zero_case, -1 - n_kt * tq, key_min))
    clo0 = jnp.where(pos_case, cnt_pos, jnp.where(zero_case, cnt_nonneg, q_pos + 1))
    hi0 = jnp.where(pos_case, key_max + 1, jnp.where(zero_case, 0, ZERO_BAND_MIN))
    chi0 = jnp.where(pos_case, 0, jnp.where(zero_case, cnt_pos, cnt_nonneg))

    def n_active(lo, hi, clo):
        return jnp.max(jnp.where((clo != k_row) & (hi - 1 > lo), 1, 0))

    def search_cond(carry):
        it, nact = carry[0], carry[1]
        return (nact > 0) & (it < SEARCH_VALUE_ROUNDS + 33)

    def search_round(it, lo, hi, clo, chi, bisect):
        active = (clo != k_row) & (hi - 1 > lo)
        lov = _key_to_float(lo)
        hiv = _key_to_float(hi)
        log_clo = jnp.log(clo.astype(F32) + 0.5)
        frac_interp = (log_clo - log_k) / (log_clo - jnp.log(chi.astype(F32) + 0.5))
        frac = jnp.where(bisect, 0.5, frac_interp)
        cand_value = _float_to_key(lov + (hiv - lov) * frac)
        cand_key = lo + ((hi - lo) >> 1)
        cand = jnp.where((it >= SEARCH_VALUE_ROUNDS) | zero_case, cand_key, cand_value)
        cand = jnp.minimum(jnp.maximum(cand, lo + 1), hi - 1)
        cand = jnp.where(active, cand, lo)
        cnt = count(lambda ky: ky >= cand)
        ok = cnt >= k_row
        return (jnp.where(ok, cand, lo), jnp.where(ok, hi, cand),
                jnp.where(ok, cnt, clo), jnp.where(ok, chi, cnt))

    log_k = jnp.log(k_row.astype(F32) + 0.5)

    def search_body(carry):
        it, _, lo, hi, clo, chi = carry
        group = it // SEARCH_ROUNDS_PER_CHECK
        for r in range(SEARCH_ROUNDS_PER_CHECK):
            if r == SEARCH_ROUNDS_PER_CHECK - 1:
                bisect = (group > 0) & (group % 2 == 0)
            else:
                bisect = group == 0
            lo, hi, clo, chi = search_round(it, lo, hi, clo, chi, bisect)
        return it + SEARCH_ROUNDS_PER_CHECK, n_active(lo, hi, clo), lo, hi, clo, chi

    _, _, thr, _, nge, _ = lax.while_loop(
        search_cond, search_body, (jnp.int32(0), n_active(lo0, hi0, clo0), lo0, hi0, clo0, chi0))

    cut_sc[...] = jnp.full((1, tq), n_kt * tq, jnp.int32)
    has_ties = jnp.max(nge - k_row) > 0

    @pl.when(has_ties)
    def _():
        n_gt = count(lambda ky: ky > thr)
        need = (k_row - n_gt).astype(F32)
        tri = jnp.where(lax.broadcasted_iota(jnp.int32, (tq, tq), 1) <= key_off, 1.0, 0.0).astype(BF16)

        def tie_tile(kt, carry):
            seen, best = carry
            eq = slab[kt] == thr
            rank = jnp.dot(tri, jnp.where(eq, 1.0, 0.0).astype(BF16),
                           preferred_element_type=F32) + seen
            hit = eq & (rank == need)
            best = jnp.maximum(best, fold(jnp.where(hit, kt * tq + key_off, -1), jnp.max))
            return rank[tq - 1:tq, :], best

        _, best = lax.fori_loop(0, n_kt, tie_tile,
                                (jnp.zeros((1, tq), F32), jnp.full(part, -1, jnp.int32)))
        cut_sc[...] = jnp.max(best, axis=0, keepdims=True)

    cut = cut_sc[...]

    m_sc[...] = jnp.full(m_sc.shape, NEG_BIG, F32)
    acc_sc[...] = jnp.zeros(acc_sc.shape, F32)

    n_steps = A_HEADS // FLASH_HEADS

    def mask_bias(kt):
        keys = slab[kt]
        sel = (keys > thr) | ((keys == thr) & ((kt * tq + key_off) <= cut))
        return jnp.where(sel, 0.0, NEG_BIG).astype(BF16)

    def logits(kt, st, bias):
        qs = q_ref[st * FLASH_HEADS:(st + 1) * FLASH_HEADS].reshape(FLASH_HEADS * tq, A_HEAD_DIM)
        kg = k_ref[st * FLASH_HEADS // A_REP, pl.ds(pl.multiple_of(kt * tq, tq), tq), :]
        s = lax.dot_general(kg, qs, _NT, preferred_element_type=F32).astype(BF16)
        return s + jnp.concatenate([bias] * FLASH_HEADS, axis=1)

    def attn_tile(kt, carry):
        s_next, bias = carry
        for st in range(n_steps):
            s = s_next
            if st + 1 < n_steps:
                s_next = logits(kt, st + 1, bias)
            else:
                kt_next = jnp.minimum(kt + 1, n_kt - 1)
                bias = mask_bias(kt_next)
                s_next = logits(kt_next, 0, bias)
            m_prev = m_sc[st]
            m_new = jnp.maximum(m_prev, jnp.max(s, axis=0, keepdims=True).astype(F32))
            alpha = jnp.exp2(m_prev - m_new)
            p = jnp.exp2(s - m_new.astype(BF16))
            acc_sc[st] = alpha * acc_sc[st] + jnp.dot(vt_ref[st * FLASH_HEADS // A_REP, kt], p,
                                                      preferred_element_type=F32)
            m_sc[st] = m_new
        return s_next, bias

    bias0 = mask_bias(0)
    lax.fori_loop(0, n_kt, attn_tile, (logits(0, 0, bias0), bias0))

    for st in range(n_steps):
        acc = acc_sc[st]
        og = acc[:A_HEAD_DIM] / acc[A_HEAD_DIM:A_HEAD_DIM + 1]
        for r in range(FLASH_HEADS):
            hh = st * FLASH_HEADS + r
            ot_sc[hh * A_HEAD_DIM:(hh + 1) * A_HEAD_DIM, :] = og[:, r * tq:(r + 1) * tq]
    o_ref[...] = ot_sc[...].T.astype(BF16)


def _dsa_attn(q, k, vt, iq, ik, iwt, *, batch, seq, tq):
    t = batch * seq
    nq = seq // tq
    top_k = min(IDX_TOPK_MAX, seq // 4)
    qtile = lambda b, i: (0, b * nq + i, 0)
    whole = lambda b, i: (0, b, 0)
    once = pl.Buffered(1)
    return pl.pallas_call(
        functools.partial(_dsa_attn_kernel, tq=tq, n_qt=nq, top_k=top_k),
        grid=(batch, nq),
        in_specs=[pl.BlockSpec((A_HEADS, tq, A_HEAD_DIM), qtile),
                  pl.BlockSpec((IDX_HEADS, tq, IDX_DIM), qtile),
                  pl.BlockSpec((IDX_HEADS, tq), lambda b, i: (0, b * nq + i)),
                  pl.BlockSpec((A_KV_HEADS, seq, A_HEAD_DIM), whole, pipeline_mode=once),
                  pl.BlockSpec((A_KV_HEADS, nq, PV_ROWS, tq), lambda b, i: (0, b, 0, 0),
                               pipeline_mode=once),
                  pl.BlockSpec((seq, IDX_DIM), lambda b, i: (b, 0), pipeline_mode=once)],
        out_specs=pl.BlockSpec((tq, A_Q), lambda b, i: (b * nq + i, 0)),
        out_shape=jax.ShapeDtypeStruct((t, A_Q), BF16),
        scratch_shapes=[pltpu.VMEM((nq + nq % 2, tq, tq), jnp.int32),
                        pltpu.VMEM((A_HEADS // FLASH_HEADS, 1, FLASH_HEADS * tq), F32),
                        pltpu.VMEM((A_HEADS // FLASH_HEADS, PV_ROWS, FLASH_HEADS * tq), F32),
                        pltpu.VMEM((1, tq), jnp.int32),
                        pltpu.VMEM((A_Q, tq), F32)],
        compiler_params=_params("parallel", "arbitrary"),
        name="dsa_attn",
    )(q, iq, iwt, k, vt, ik)


def _mix_mlp_kernel(y_ref, wo_ref, h_ref, g1_ref, b1_ref, wu_ref, wd_ref, g2_ref, b2_ref, o_ref, *, tf):
    mix = jnp.dot(y_ref[...], wo_ref[...], preferred_element_type=F32)
    h = _layer_norm(DN_ALPHA * h_ref[...] + mix, g1_ref[...], b1_ref[...])
    xb = h.astype(BF16)
    acc = jnp.zeros(h.shape, F32)
    for c in range(D_FF // tf):
        u = jnp.dot(xb, wu_ref[:, c * tf:(c + 1) * tf], preferred_element_type=F32)
        u = jnp.square(jnp.maximum(u, 0.0)).astype(BF16)
        acc = acc + jnp.dot(u, wd_ref[c * tf:(c + 1) * tf, :], preferred_element_type=F32)
    o_ref[...] = _layer_norm(DN_ALPHA * h + acc, g2_ref[...], b2_ref[...])


def _mix_mlp(y, wo, h, g1, b1, wu, wd, g2, b2, *, tm, tf):
    t, kdim = y.shape
    row = lambda i: (i, 0)
    once = pl.Buffered(1)
    resident = lambda w: pl.BlockSpec(w.shape, lambda i: (0, 0), pipeline_mode=once)
    return pl.pallas_call(
        functools.partial(_mix_mlp_kernel, tf=tf),
        grid=(t // tm,),
        in_specs=[pl.BlockSpec((tm, kdim), row), resident(wo), pl.BlockSpec((tm, D_MODEL), row),
                  _const_spec(g1.shape), _const_spec(b1.shape), resident(wu), resident(wd),
                  _const_spec(g2.shape), _const_spec(b2.shape)],
        out_specs=pl.BlockSpec((tm, D_MODEL), row),
        out_shape=jax.ShapeDtypeStruct((t, D_MODEL), F32),
        compiler_params=_params("parallel"),
        name="mix_mlp",
    )(y, wo, h, g1, b1, wu, wd, g2, b2)


def _gla_proj_kernel(h_ref, wq_ref, wk_ref, wv_ref, wr_ref, wa_ref,
                     q_ref, k_ref, v_ref, r_ref, a_ref):
    xb = h_ref[...].astype(BF16)
    q = jnp.dot(xb, wq_ref[...], preferred_element_type=F32)
    q_ref[...] = (q * (B_DK ** -0.5)).astype(BF16)
    k_ref[...] = jnp.dot(xb, wk_ref[...], preferred_element_type=F32).astype(BF16)
    v_ref[...] = jnp.dot(xb, wv_ref[...], preferred_element_type=F32).astype(BF16)
    r_ref[...] = jnp.dot(xb, wr_ref[...], preferred_element_type=F32).astype(BF16)
    a = jnp.dot(xb, wa_ref[...], preferred_element_type=F32)
    a_ref[...] = a[:, :B_GATE_RANK]


def _gla_proj(h, wq, wk, wv, wr, wa, *, tm):
    t = h.shape[0]
    row = lambda i: (i, 0)
    return pl.pallas_call(
        _gla_proj_kernel,
        grid=(t // tm,),
        in_specs=[pl.BlockSpec((tm, D_MODEL), row), _const_spec(wq.shape), _const_spec(wk.shape),
                  _const_spec(wv.shape), _const_spec(wr.shape), _const_spec(wa.shape)],
        out_specs=[pl.BlockSpec((tm, B_QK), row), pl.BlockSpec((tm, B_QK), row),
                   pl.BlockSpec((tm, B_V), row), pl.BlockSpec((tm, B_V), row),
                   pl.BlockSpec((tm, B_GATE_RANK), row)],
        out_shape=[jax.ShapeDtypeStruct((t, B_QK), BF16), jax.ShapeDtypeStruct((t, B_QK), BF16),
                   jax.ShapeDtypeStruct((t, B_V), BF16), jax.ShapeDtypeStruct((t, B_V), BF16),
                   jax.ShapeDtypeStruct((t, B_GATE_RANK), F32)],
        compiler_params=_params("parallel"),
        name="gla_proj",
    )(h, wq, wk, wv, wr, wa)


def _gla_kernel(q_ref, k_ref, v_ref, r_ref, a_ref, w2_ref, ba_ref, gn_ref, y_ref, st_ref, *, rows, heads):
    n_chunks = rows // B_CHUNK

    @pl.when(pl.program_id(2) == 0)
    def _():
        st_ref[...] = jnp.zeros(st_ref.shape, F32)

    x = jnp.dot(a_ref[...], w2_ref[...], preferred_element_type=F32,
                precision=lax.Precision.HIGHEST) + ba_ref[...]
    log_a = (jnp.minimum(x, 0.0) - jnp.log1p(jnp.exp(-jnp.abs(x)))) / B_GATE_TAU

    row_in_chunk = lax.broadcasted_iota(jnp.int32, log_a.shape, 0) % B_CHUNK
    b = log_a
    step = 1
    while step < B_CHUNK:
        b = b + jnp.where(row_in_chunk >= step, pltpu.roll(b, step, axis=0), 0.0)
        step *= 2

    tril = (lax.broadcasted_iota(jnp.int32, (B_CHUNK, B_CHUNK), 1)
            <= lax.broadcasted_iota(jnp.int32, (B_CHUNK, B_CHUNK), 0))
    gn = gn_ref[...]

    prep = []
    for hd in range(heads):
        ks = slice(hd * B_DK, (hd + 1) * B_DK)
        b3 = b[:, ks].reshape(n_chunks, B_CHUNK, B_DK)
        b_last = b3[:, B_CHUNK - 1:B_CHUNK, :]
        b_mid = b3[:, B_CHUNK // 2 - 1:B_CHUNK // 2, :]
        q3 = q_ref[:, ks].astype(F32).reshape(n_chunks, B_CHUNK, B_DK)
        k3 = k_ref[:, ks].astype(F32).reshape(n_chunks, B_CHUNK, B_DK)
        prep.append(dict(
            q_in=(q3 * jnp.exp(b3)).astype(BF16),
            k_out=(k3 * jnp.exp(b_last - b3)).astype(BF16),
            q_mid=(q3 * jnp.exp(jnp.minimum(b3 - b_mid, EXP_CLAMP))).astype(BF16),
            k_mid=(k3 * jnp.exp(jnp.minimum(b_mid - b3, EXP_CLAMP))).astype(BF16),
            decay=jnp.exp(b_last),
            state=st_ref[hd]))

    for n in range(n_chunks):
        lo, hi = n * B_CHUNK, (n + 1) * B_CHUNK
        for hd in range(heads):
            p = prep[hd]
            vs = slice(hd * B_DV, (hd + 1) * B_DV)
            vn = v_ref[lo:hi, vs]
            attn = lax.dot_general(p["q_mid"][n], p["k_mid"][n], _NT, preferred_element_type=F32)
            attn = jnp.where(tril, attn, 0.0).astype(BF16)
            o = jnp.dot(attn, vn, preferred_element_type=F32)
            o = o + lax.dot_general(p["q_in"][n], p["state"].astype(BF16), _NT,
                                    preferred_element_type=F32)
            upd = lax.dot_general(vn, p["k_out"][n], _TN, preferred_element_type=F32)
            p["state"] = p["decay"][n] * p["state"] + upd
            o = o * lax.rsqrt(jnp.mean(o * o, axis=-1, keepdims=True) + RMS_EPS) * gn
            rn = r_ref[lo:hi, vs].astype(F32)
            gate = rn / (1.0 + jnp.exp(-rn))
            y_ref[lo:hi, vs] = (o * gate).astype(BF16)
    for hd in range(heads):
        st_ref[hd] = prep[hd]["state"]


def _gla(q, k, v, r, a, w2, ba, gn, *, batch, seq, rows, heads):
    t = batch * seq
    nb = seq // rows
    tile = lambda b, h, j: (b * nb + j, h)
    return pl.pallas_call(
        functools.partial(_gla_kernel, rows=rows, heads=heads),
        grid=(batch, B_HEADS // heads, nb),
        in_specs=[pl.BlockSpec((rows, heads * B_DK), tile), pl.BlockSpec((rows, heads * B_DK), tile),
                  pl.BlockSpec((rows, heads * B_DV), tile), pl.BlockSpec((rows, heads * B_DV), tile),
                  pl.BlockSpec((rows, B_GATE_RANK), lambda b, h, j: (b * nb + j, 0)),
                  pl.BlockSpec((B_GATE_RANK, heads * B_DK), lambda b, h, j: (0, h)),
                  pl.BlockSpec((1, heads * B_DK), lambda b, h, j: (0, h)),
                  _const_spec(gn.shape)],
        out_specs=pl.BlockSpec((rows, heads * B_DV), tile),
        out_shape=jax.ShapeDtypeStruct((t, B_V), BF16),
        scratch_shapes=[pltpu.VMEM((heads, B_DV, B_DK), F32)],
        compiler_params=_params("parallel", "parallel", "arbitrary"),
        name="gla",
    )(q, k, v, r, a, w2, ba, gn)


def _tile(n, pref):
    return pref if n % pref == 0 else n


def _forward(x, positions, a_w_in, a_w_o, b_w_in, b_w_a2, b_b_a, b_g_norm, b_w_o,
             ln_mix_g, ln_mix_b, mlp_w_up, mlp_w_down, ln_mlp_g, ln_mlp_b):
    batch, seq, _ = x.shape
    t = batch * seq
    tm = _tile(t, 512)
    tq = _tile(seq, 256)
    rows = _tile(seq, 512)

    h = x.reshape(t, D_MODEL)
    pos = positions.reshape(t, 1).astype(F32)
    inv = ROPE_THETA ** (-jnp.arange(0, ROT_DIM, 2, dtype=F32) / ROT_DIM)
    inv = jnp.concatenate([inv, inv, jnp.zeros((A_HEAD_DIM - ROT_DIM,), F32)])
    inv = jnp.tile(inv, LANES // A_HEAD_DIM).reshape(1, LANES)
    row = lambda p: p.reshape(1, -1)

    for layer in range(DEPTH):
        j = layer // 2
        if layer % 2 == 0:
            w = a_w_in[j].astype(BF16)
            o1, o2, o3, o4 = A_Q, A_Q + A_KV, A_Q + 2 * A_KV, A_Q + 2 * A_KV + A_IQ
            o5 = o4 + IDX_DIM
            wik = jnp.pad(w[:, o4:o5], ((0, 0), (0, LANES - IDX_DIM)))
            wiwt = jnp.pad(w[:, o5:].T, ((0, 16 - IDX_HEADS), (0, 0)))
            q, k, vt, iq, ik, iwt = _dsa_proj(h, pos, inv, w[:, :o1], w[:, o1:o2], w[:, o2:o3].T,
                                              w[:, o3:o4], wik, wiwt, tm=tm, tkv=tq)
            y = _dsa_attn(q, k, vt, iq, ik, iwt, batch=batch, seq=seq, tq=tq)
            w_o = a_w_o[j].astype(BF16)
        else:
            w = b_w_in[j].astype(BF16)
            o1, o2, o3, o4 = B_QK, 2 * B_QK, 2 * B_QK + B_V, 2 * B_QK + 2 * B_V
            wa = jnp.pad(w[:, o4:], ((0, 0), (0, LANES - B_GATE_RANK)))
            q, k, v, r, a = _gla_proj(h, w[:, :o1], w[:, o1:o2], w[:, o2:o3], w[:, o3:o4], wa, tm=tm)
            y = _gla(q, k, v, r, a, b_w_a2[j], row(b_b_a[j]), row(b_g_norm[j]),
                     batch=batch, seq=seq, rows=rows, heads=GLA_HEADS_PER_STEP)
            w_o = b_w_o[j].astype(BF16)
        h = _mix_mlp(y, w_o, h, row(ln_mix_g[layer]), row(ln_mix_b[layer]),
                     mlp_w_up[layer].astype(BF16), mlp_w_down[layer].astype(BF16),
                     row(ln_mlp_g[layer]), row(ln_mlp_b[layer]), tm=tm, tf=1024)
    return h.reshape(batch, seq, D_MODEL)


_forward_jit = jax.jit(_forward)


def kernel(x, positions, a_w_in, a_w_o, b_w_in, b_w_a2, b_b_a, b_g_norm, b_w_o,
           ln_mix_g, ln_mix_b, mlp_w_up, mlp_w_down, ln_mlp_g, ln_mlp_b):
    return _forward_jit(x, positions, a_w_in, a_w_o, b_w_in, b_w_a2, b_b_a, b_g_norm, b_w_o,
                        ln_mix_g, ln_mix_b, mlp_w_up, mlp_w_down, ln_mlp_g, ln_mlp_b)
```

```python
import functools

import jax
import jax.numpy as jnp
from jax import lax
from jax.experimental import pallas as pl
from jax.experimental.pallas import tpu as pltpu

D_MODEL = 1024
DEPTH = 4

A_HEADS = 16
A_KV_HEADS = 4
A_HEAD_DIM = 64
A_REP = A_HEADS // A_KV_HEADS
IDX_HEADS = 8
IDX_DIM = 64
IDX_TOPK_MAX = 256
A_Q = A_HEADS * A_HEAD_DIM
A_KV = A_KV_HEADS * A_HEAD_DIM
A_IQ = IDX_HEADS * IDX_DIM

B_HEADS = 4
B_DK = D_MODEL // 2 // B_HEADS
B_DV = D_MODEL // B_HEADS
B_GATE_RANK = 16
B_GATE_TAU = 16.0
B_CHUNK = 64
B_QK = B_HEADS * B_DK
B_V = B_HEADS * B_DV

D_FF = 4 * D_MODEL

ROPE_THETA = 500000.0
ROT_DIM = A_HEAD_DIM // 4
ROT_HALF = ROT_DIM // 2
LN_EPS = 1e-5
RMS_EPS = 1e-6
DN_ALPHA = (2 * DEPTH) ** 0.25

LANES = 128
VMEM_LIMIT_BYTES = 56 * 1024 * 1024
INT32_MIN = -(2 ** 31)
INT32_MAX = 2 ** 31 - 1
ZERO_BAND_MIN = -(2 ** 23)
NEG_BIG = -(2.0 ** 100)
EXP_CLAMP = 80.0
LOG2_E = 1.4426950408889634
PV_ROWS = A_HEAD_DIM + 16
SEARCH_VALUE_ROUNDS = 24
SEARCH_ROUNDS_PER_CHECK = 3
FLASH_HEADS = 4
GLA_HEADS_PER_STEP = 4
FOLD_ROWS = 32

BF16 = jnp.bfloat16
F32 = jnp.float32

_NT = (((1,), (1,)), ((), ()))
_TN = (((0,), (0,)), ((), ()))


def _params(*sem):
    return pltpu.CompilerParams(dimension_semantics=sem,
                                vmem_limit_bytes=VMEM_LIMIT_BYTES)


def _const_spec(shape):
    nd = len(shape)
    return pl.BlockSpec(shape, lambda *_: (0,) * nd)


def _float_to_key(x):
    bits = lax.bitcast_convert_type(x, jnp.int32)
    return bits ^ ((bits >> 31) & INT32_MAX)


def _key_to_float(key):
    return lax.bitcast_convert_type(key ^ ((key >> 31) & INT32_MAX), F32)


def _layer_norm(z, g, b):
    mu = jnp.mean(z, axis=-1, keepdims=True)
    zc = z - mu
    var = jnp.mean(zc * zc, axis=-1, keepdims=True)
    return zc * lax.rsqrt(var + LN_EPS) * g + b


def _rot_tables(pos, inv):
    ang = pos * inv
    c = jnp.cos(ang)
    s = jnp.sin(ang)
    lane = lax.broadcasted_iota(jnp.int32, ang.shape, 1) % A_HEAD_DIM
    s_lo = jnp.where(lane < ROT_HALF, -s, 0.0)
    s_hi = jnp.where((lane >= ROT_HALF) & (lane < ROT_DIM), s, 0.0)
    return c, s_lo, s_hi


def _rotate(t, c, s_lo, s_hi):
    w = t.shape[1]
    reps = w // LANES
    if reps > 1:
        c = jnp.concatenate([c] * reps, axis=1)
        s_lo = jnp.concatenate([s_lo] * reps, axis=1)
        s_hi = jnp.concatenate([s_hi] * reps, axis=1)
    up = pltpu.roll(t, w - ROT_HALF, axis=1)
    dn = pltpu.roll(t, ROT_HALF, axis=1)
    return t * c + up * s_lo + dn * s_hi


def _dsa_proj_kernel(h_ref, pos_ref, inv_ref, wq_ref, wk_ref, wvt_ref, wiq_ref, wik_ref, wiwt_ref,
                     q_ref, k_ref, vt_ref, iq_ref, ik_ref, iwt_ref, *, tkv):
    xb = h_ref[...].astype(BF16)
    tm = xb.shape[0]
    c, s_lo, s_hi = _rot_tables(pos_ref[...], inv_ref[...])

    q = jnp.dot(xb, wq_ref[...], preferred_element_type=F32)
    q = _rotate(q, c, s_lo, s_hi) * (A_HEAD_DIM ** -0.5 * LOG2_E)
    for hh in range(A_HEADS):
        q_ref[hh] = q[:, hh * A_HEAD_DIM:(hh + 1) * A_HEAD_DIM].astype(BF16)

    k = jnp.dot(xb, wk_ref[...], preferred_element_type=F32)
    k = _rotate(k, c, s_lo, s_hi)
    vt = lax.dot_general(wvt_ref[...], xb, _NT, preferred_element_type=F32)
    ones_row = jnp.where(lax.broadcasted_iota(jnp.int32, (PV_ROWS - A_HEAD_DIM, tkv), 0) == 0, 1.0, 0.0)
    for g in range(A_KV_HEADS):
        k_ref[g] = k[:, g * A_HEAD_DIM:(g + 1) * A_HEAD_DIM].astype(BF16)
        for j in range(tm // tkv):
            vt_gj = vt[g * A_HEAD_DIM:(g + 1) * A_HEAD_DIM, j * tkv:(j + 1) * tkv]
            vt_ref[g, j] = jnp.concatenate([vt_gj, ones_row], axis=0).astype(BF16)

    iq = jnp.dot(xb, wiq_ref[...], preferred_element_type=F32)
    iq = _rotate(iq, c, s_lo, s_hi) * (IDX_DIM ** -0.5)
    for hh in range(IDX_HEADS):
        iq_ref[hh] = iq[:, hh * IDX_DIM:(hh + 1) * IDX_DIM].astype(BF16)

    x = jnp.dot(xb, wik_ref[...], preferred_element_type=F32)
    first = lax.broadcasted_iota(jnp.int32, x.shape, 1) < IDX_DIM
    x = _rotate(x, jnp.where(first, c, 1.0), jnp.where(first, s_lo, 0.0), jnp.where(first, s_hi, 0.0))
    ik_ref[...] = x[:, :IDX_DIM].astype(BF16)
    iwt = lax.dot_general(wiwt_ref[...], xb, _NT, preferred_element_type=F32)
    iwt_ref[...] = iwt[:IDX_HEADS] * (IDX_HEADS ** -0.5)


def _dsa_proj(h, pos, inv, wq, wk, wvt, wiq, wik, wiwt, *, tm, tkv):
    t = h.shape[0]
    row = lambda i: (i, 0)
    head = lambda i: (0, i, 0)
    return pl.pallas_call(
        functools.partial(_dsa_proj_kernel, tkv=tkv),
        grid=(t // tm,),
        in_specs=[pl.BlockSpec((tm, D_MODEL), row), pl.BlockSpec((tm, 1), row),
                  _const_spec(inv.shape), _const_spec(wq.shape), _const_spec(wk.shape),
                  _const_spec(wvt.shape), _const_spec(wiq.shape), _const_spec(wik.shape),
                  _const_spec(wiwt.shape)],
        out_specs=[pl.BlockSpec((A_HEADS, tm, A_HEAD_DIM), head),
                   pl.BlockSpec((A_KV_HEADS, tm, A_HEAD_DIM), head),
                   pl.BlockSpec((A_KV_HEADS, tm // tkv, PV_ROWS, tkv), lambda i: (0, i, 0, 0)),
                   pl.BlockSpec((IDX_HEADS, tm, IDX_DIM), head),
                   pl.BlockSpec((tm, IDX_DIM), row),
                   pl.BlockSpec((IDX_HEADS, tm), lambda i: (0, i))],
        out_shape=[jax.ShapeDtypeStruct((A_HEADS, t, A_HEAD_DIM), BF16),
                   jax.ShapeDtypeStruct((A_KV_HEADS, t, A_HEAD_DIM), BF16),
                   jax.ShapeDtypeStruct((A_KV_HEADS, t // tkv, PV_ROWS, tkv), BF16),
                   jax.ShapeDtypeStruct((IDX_HEADS, t, IDX_DIM), BF16),
                   jax.ShapeDtypeStruct((t, IDX_DIM), BF16),
                   jax.ShapeDtypeStruct((IDX_HEADS, t), F32)],
        compiler_params=_params("parallel"),
        name="dsa_proj",
    )(h, pos, inv, wq, wk, wvt, wiq, wik, wiwt)


def _dsa_attn_kernel(q_ref, iq_ref, iwt_ref, k_ref, vt_ref, ik_ref, o_ref,
                     slab, m_sc, acc_sc, cut_sc, ot_sc, *, tq, n_qt, top_k):
    i = pl.program_id(1)
    n_kt = i + 1
    q_pos = i * tq + lax.broadcasted_iota(jnp.int32, (1, tq), 1)
    key_off = lax.broadcasted_iota(jnp.int32, (tq, tq), 0)

    iw = iwt_ref[...]

    def fold(x, op):
        return op(x.reshape(-1, FOLD_ROWS, tq), axis=0)

    n_pairs = (n_kt + 1) // 2

    def score_pair(j, carry):
        mx, mn, c0, c1 = carry
        for kt in (2 * j, 2 * j + 1):
            ik_t = ik_ref[pl.ds(pl.multiple_of(jnp.minimum(kt, n_qt - 1) * tq, tq), tq), :]
            sc = jnp.zeros((tq, tq), F32)
            for hh in range(IDX_HEADS):
                s = lax.dot_general(ik_t, iq_ref[hh], _NT, preferred_element_type=F32)
                sc = sc + iw[hh:hh + 1, :] * jnp.maximum(s, 0.0)
            key = _float_to_key(sc)
            idx = kt * tq + key_off
            key = jnp.where(key == 0, -1 - idx, key)
            causal = idx <= q_pos
            key = jnp.where(causal, key, INT32_MIN)
            slab[kt] = key
            mx = jnp.maximum(mx, fold(key, jnp.max))
            mn = jnp.minimum(mn, fold(jnp.where(causal, key, INT32_MAX), jnp.min))
            c0 = c0 + fold(jnp.where(key >= ZERO_BAND_MIN, 1, 0), jnp.sum)
            c1 = c1 + fold(jnp.where(key >= 1, 1, 0), jnp.sum)
        return mx, mn, c0, c1

    part = (FOLD_ROWS, tq)
    init = (jnp.full(part, INT32_MIN, jnp.int32), jnp.full(part, INT32_MAX, jnp.int32),
            jnp.zeros(part, jnp.int32), jnp.zeros(part, jnp.int32))
    mx, mn, c0, c1 = lax.fori_loop(0, n_pairs, score_pair, init)
    key_max = jnp.max(mx, axis=0, keepdims=True)
    key_min = jnp.min(mn, axis=0, keepdims=True)
    cnt_nonneg = jnp.sum(c0, axis=0, keepdims=True)
    cnt_pos = jnp.sum(c1, axis=0, keepdims=True)

    k_row = jnp.minimum(top_k, q_pos + 1)

    def count(pred):
        def body(j, acc):
            hit = jnp.where(pred(slab[pl.ds(2 * j, 2)]), 1, 0)
            return acc + fold(hit, jnp.sum)
        acc = lax.fori_loop(0, n_pairs, body, jnp.zeros(part, jnp.int32))
        return jnp.sum(acc, axis=0, keepdims=True)

    pos_case = cnt_pos >= k_row
    zero_case = jnp.logical_not(pos_case) & (cnt_nonneg >= k_row)
    lo0 = jnp.where(pos_case, 1, jnp.where(zero_case, -1 - n_kt * tq, key_min))
    clo0 = jnp.where(pos_case, cnt_pos, jnp.where(zero_case, cnt_nonneg, q_pos + 1))
    hi0 = jnp.where(pos_case, key_max + 1, jnp.where(zero_case, 0, ZERO_BAND_MIN))
    chi0 = jnp.where(pos_case, 0, jnp.where(zero_case, cnt_pos, cnt_nonneg))

    def n_active(lo, hi, clo):
        return jnp.max(jnp.where((clo != k_row) & (hi - 1 > lo), 1, 0))

    def search_cond(carry):
        it, nact = carry[0], carry[1]
        return (nact > 0) & (it < SEARCH_VALUE_ROUNDS + 33)

    def search_round(it, lo, hi, clo, chi, bisect):
        active = (clo != k_row) & (hi - 1 > lo)
        lov = _key_to_float(lo)
        hiv = _key_to_float(hi)
        log_clo = jnp.log(clo.astype(F32) + 0.5)
        frac_interp = (log_clo - log_k) / (log_clo - jnp.log(chi.astype(F32) + 0.5))
        frac = jnp.where(bisect, 0.5, frac_interp)
        cand_value = _float_to_key(lov + (hiv - lov) * frac)
        cand_key = lo + ((hi - lo) >> 1)
        cand = jnp.where((it >= SEARCH_VALUE_ROUNDS) | zero_case, cand_key, cand_value)
        cand = jnp.minimum(jnp.maximum(cand, lo + 1), hi - 1)
        cand = jnp.where(active, cand, lo)
        cnt = count(lambda ky: ky >= cand)
        ok = cnt >= k_row
        return (jnp.where(ok, cand, lo), jnp.where(ok, hi, cand),
                jnp.where(ok, cnt, clo), jnp.where(ok, chi, cnt))

    log_k = jnp.log(k_row.astype(F32) + 0.5)

    def search_body(carry):
        it, _, lo, hi, clo, chi = carry
        group = it // SEARCH_ROUNDS_PER_CHECK
        for r in range(SEARCH_ROUNDS_PER_CHECK):
            if r == SEARCH_ROUNDS_PER_CHECK - 1:
                bisect = (group > 0) & (group % 2 == 0)
            else:
                bisect = group == 0
            lo, hi, clo, chi = search_round(it, lo, hi, clo, chi, bisect)
        return it + SEARCH_ROUNDS_PER_CHECK, n_active(lo, hi, clo), lo, hi, clo, chi

    _, _, thr, _, nge, _ = lax.while_loop(
        search_cond, search_body, (jnp.int32(0), n_active(lo0, hi0, clo0), lo0, hi0, clo0, chi0))

    cut_sc[...] = jnp.full((1, tq), n_kt * tq, jnp.int32)
    has_ties = jnp.max(nge - k_row) > 0

    @pl.when(has_ties)
    def _():
        n_gt = count(lambda ky: ky > thr)
        need = (k_row - n_gt).astype(F32)
        tri = jnp.where(lax.broadcasted_iota(jnp.int32, (tq, tq), 1) <= key_off, 1.0, 0.0).astype(BF16)

        def tie_tile(kt, carry):
            seen, best = carry
            eq = slab[kt] == thr
            rank = jnp.dot(tri, jnp.where(eq, 1.0, 0.0).astype(BF16),
                           preferred_element_type=F32) + seen
            hit = eq & (rank == need)
            best = jnp.maximum(best, fold(jnp.where(hit, kt * tq + key_off, -1), jnp.max))
            return rank[tq - 1:tq, :], best

        _, best = lax.fori_loop(0, n_kt, tie_tile,
                                (jnp.zeros((1, tq), F32), jnp.full(part, -1, jnp.int32)))
        cut_sc[...] = jnp.max(best, axis=0, keepdims=True)

    cut = cut_sc[...]

    m_sc[...] = jnp.full(m_sc.shape, NEG_BIG, F32)
    acc_sc[...] = jnp.zeros(acc_sc.shape, F32)

    n_steps = A_HEADS // FLASH_HEADS

    def mask_bias(kt):
        keys = slab[kt]
        sel = (keys > thr) | ((keys == thr) & ((kt * tq + key_off) <= cut))
        return jnp.where(sel, 0.0, NEG_BIG).astype(BF16)

    def logits(kt, st, bias):
        qs = q_ref[st * FLASH_HEADS:(st + 1) * FLASH_HEADS].reshape(FLASH_HEADS * tq, A_HEAD_DIM)
        kg = k_ref[st * FLASH_HEADS // A_REP, pl.ds(pl.multiple_of(kt * tq, tq), tq), :]
        s = lax.dot_general(kg, qs, _NT, preferred_element_type=F32).astype(BF16)
        return s + jnp.concatenate([bias] * FLASH_HEADS, axis=1)

    def attn_tile(kt, carry):
        s_next, bias = carry
        for st in range(n_steps):
            s = s_next
            if st + 1 < n_steps:
                s_next = logits(kt, st + 1, bias)
            else:
                kt_next = jnp.minimum(kt + 1, n_kt - 1)
                bias = mask_bias(kt_next)
                s_next = logits(kt_next, 0, bias)
            m_prev = m_sc[st]
            m_new = jnp.maximum(m_prev, jnp.max(s, axis=0, keepdims=True).astype(F32))
            alpha = jnp.exp2(m_prev - m_new)
            p = jnp.exp2(s - m_new.astype(BF16))
            acc_sc[st] = alpha * acc_sc[st] + jnp.dot(vt_ref[st * FLASH_HEADS // A_REP, kt], p,
                                                      preferred_element_type=F32)
            m_sc[st] = m_new
        return s_next, bias

    def attn_pair(j, carry):
        return attn_tile(2 * j + 1, attn_tile(2 * j, carry))

    bias0 = mask_bias(0)
    s_last, bias_last = lax.fori_loop(0, n_kt // 2, attn_pair, (logits(0, 0, bias0), bias0))

    @pl.when(n_kt % 2 == 1)
    def _():
        attn_tile(n_kt - 1, (s_last, bias_last))

    for st in range(n_steps):
        acc = acc_sc[st]
        og = acc[:A_HEAD_DIM] / acc[A_HEAD_DIM:A_HEAD_DIM + 1]
        for r in range(FLASH_HEADS):
            hh = st * FLASH_HEADS + r
            ot_sc[hh * A_HEAD_DIM:(hh + 1) * A_HEAD_DIM, :] = og[:, r * tq:(r + 1) * tq]
    o_ref[...] = ot_sc[...].T.astype(BF16)


def _dsa_attn(q, k, vt, iq, ik, iwt, *, batch, seq, tq):
    t = batch * seq
    nq = seq // tq
    top_k = min(IDX_TOPK_MAX, seq // 4)
    qtile = lambda b, i: (0, b * nq + i, 0)
    whole = lambda b, i: (0, b, 0)
    once = pl.Buffered(1)
    return pl.pallas_call(
        functools.partial(_dsa_attn_kernel, tq=tq, n_qt=nq, top_k=top_k),
        grid=(batch, nq),
        in_specs=[pl.BlockSpec((A_HEADS, tq, A_HEAD_DIM), qtile),
                  pl.BlockSpec((IDX_HEADS, tq, IDX_DIM), qtile),
                  pl.BlockSpec((IDX_HEADS, tq), lambda b, i: (0, b * nq + i)),
                  pl.BlockSpec((A_KV_HEADS, seq, A_HEAD_DIM), whole, pipeline_mode=once),
                  pl.BlockSpec((A_KV_HEADS, nq, PV_ROWS, tq), lambda b, i: (0, b, 0, 0),
                               pipeline_mode=once),
                  pl.BlockSpec((seq, IDX_DIM), lambda b, i: (b, 0), pipeline_mode=once)],
        out_specs=pl.BlockSpec((tq, A_Q), lambda b, i: (b * nq + i, 0)),
        out_shape=jax.ShapeDtypeStruct((t, A_Q), BF16),
        scratch_shapes=[pltpu.VMEM((nq + nq % 2, tq, tq), jnp.int32),
                        pltpu.VMEM((A_HEADS // FLASH_HEADS, 1, FLASH_HEADS * tq), F32),
                        pltpu.VMEM((A_HEADS // FLASH_HEADS, PV_ROWS, FLASH_HEADS * tq), F32),
                        pltpu.VMEM((1, tq), jnp.int32),
                        pltpu.VMEM((A_Q, tq), F32)],
        compiler_params=_params("parallel", "arbitrary"),
        name="dsa_attn",
    )(q, iq, iwt, k, vt, ik)


def _mix_mlp_kernel(y_ref, wo_ref, h_ref, g1_ref, b1_ref, wu_ref, wd_ref, g2_ref, b2_ref, o_ref, *, tf):
    mix = jnp.dot(y_ref[...], wo_ref[...], preferred_element_type=F32)
    h = _layer_norm(DN_ALPHA * h_ref[...] + mix, g1_ref[...], b1_ref[...])
    xb = h.astype(BF16)
    acc = jnp.zeros(h.shape, F32)
    for c in range(D_FF // tf):
        u = jnp.dot(xb, wu_ref[:, c * tf:(c + 1) * tf], preferred_element_type=F32)
        u = jnp.square(jnp.maximum(u, 0.0)).astype(BF16)
        acc = acc + jnp.dot(u, wd_ref[c * tf:(c + 1) * tf, :], preferred_element_type=F32)
    o_ref[...] = _layer_norm(DN_ALPHA * h + acc, g2_ref[...], b2_ref[...])


def _mix_mlp(y, wo, h, g1, b1, wu, wd, g2, b2, *, tm, tf):
    t, kdim = y.shape
    row = lambda i: (i, 0)
    once = pl.Buffered(1)
    resident = lambda w: pl.BlockSpec(w.shape, lambda i: (0, 0), pipeline_mode=once)
    return pl.pallas_call(
        functools.partial(_mix_mlp_kernel, tf=tf),
        grid=(t // tm,),
        in_specs=[pl.BlockSpec((tm, kdim), row), resident(wo), pl.BlockSpec((tm, D_MODEL), row),
                  _const_spec(g1.shape), _const_spec(b1.shape), resident(wu), resident(wd),
                  _const_spec(g2.shape), _const_spec(b2.shape)],
        out_specs=pl.BlockSpec((tm, D_MODEL), row),
        out_shape=jax.ShapeDtypeStruct((t, D_MODEL), F32),
        compiler_params=_params("parallel"),
        name="mix_mlp",
    )(y, wo, h, g1, b1, wu, wd, g2, b2)


def _gla_proj_kernel(h_ref, wq_ref, wk_ref, wv_ref, wr_ref, wa_ref,
                     q_ref, k_ref, v_ref, r_ref, a_ref):
    xb = h_ref[...].astype(BF16)
    q = jnp.dot(xb, wq_ref[...], preferred_element_type=F32)
    q_ref[...] = (q * (B_DK ** -0.5)).astype(BF16)
    k_ref[...] = jnp.dot(xb, wk_ref[...], preferred_element_type=F32).astype(BF16)
    v_ref[...] = jnp.dot(xb, wv_ref[...], preferred_element_type=F32).astype(BF16)
    r_ref[...] = jnp.dot(xb, wr_ref[...], preferred_element_type=F32).astype(BF16)
    a = jnp.dot(xb, wa_ref[...], preferred_element_type=F32)
    a_ref[...] = a[:, :B_GATE_RANK]


def _gla_proj(h, wq, wk, wv, wr, wa, *, tm):
    t = h.shape[0]
    row = lambda i: (i, 0)
    return pl.pallas_call(
        _gla_proj_kernel,
        grid=(t // tm,),
        in_specs=[pl.BlockSpec((tm, D_MODEL), row), _const_spec(wq.shape), _const_spec(wk.shape),
                  _const_spec(wv.shape), _const_spec(wr.shape), _const_spec(wa.shape)],
        out_specs=[pl.BlockSpec((tm, B_QK), row), pl.BlockSpec((tm, B_QK), row),
                   pl.BlockSpec((tm, B_V), row), pl.BlockSpec((tm, B_V), row),
                   pl.BlockSpec((tm, B_GATE_RANK), row)],
        out_shape=[jax.ShapeDtypeStruct((t, B_QK), BF16), jax.ShapeDtypeStruct((t, B_QK), BF16),
                   jax.ShapeDtypeStruct((t, B_V), BF16), jax.ShapeDtypeStruct((t, B_V), BF16),
                   jax.ShapeDtypeStruct((t, B_GATE_RANK), F32)],
        compiler_params=_params("parallel"),
        name="gla_proj",
    )(h, wq, wk, wv, wr, wa)


def _gla_kernel(q_ref, k_ref, v_ref, r_ref, a_ref, w2_ref, ba_ref, gn_ref, y_ref, st_ref, *, rows, heads):
    n_chunks = rows // B_CHUNK

    @pl.when(pl.program_id(2) == 0)
    def _():
        st_ref[...] = jnp.zeros(st_ref.shape, F32)

    x = jnp.dot(a_ref[...], w2_ref[...], preferred_element_type=F32,
                precision=lax.Precision.HIGHEST) + ba_ref[...]
    log_a = (jnp.minimum(x, 0.0) - jnp.log1p(jnp.exp(-jnp.abs(x)))) / B_GATE_TAU

    row_in_chunk = lax.broadcasted_iota(jnp.int32, log_a.shape, 0) % B_CHUNK
    b = log_a
    step = 1
    while step < B_CHUNK:
        b = b + jnp.where(row_in_chunk >= step, pltpu.roll(b, step, axis=0), 0.0)
        step *= 2

    tril = (lax.broadcasted_iota(jnp.int32, (B_CHUNK, B_CHUNK), 1)
            <= lax.broadcasted_iota(jnp.int32, (B_CHUNK, B_CHUNK), 0))
    gn = gn_ref[...]

    prep = []
    for hd in range(heads):
        ks = slice(hd * B_DK, (hd + 1) * B_DK)
        b3 = b[:, ks].reshape(n_chunks, B_CHUNK, B_DK)
        b_last = b3[:, B_CHUNK - 1:B_CHUNK, :]
        b_mid = b3[:, B_CHUNK // 2 - 1:B_CHUNK // 2, :]
        q3 = q_ref[:, ks].astype(F32).reshape(n_chunks, B_CHUNK, B_DK)
        k3 = k_ref[:, ks].astype(F32).reshape(n_chunks, B_CHUNK, B_DK)
        prep.append(dict(
            q_in=(q3 * jnp.exp(b3)).astype(BF16),
            k_out=(k3 * jnp.exp(b_last - b3)).astype(BF16),
            q_mid=(q3 * jnp.exp(jnp.minimum(b3 - b_mid, EXP_CLAMP))).astype(BF16),
            k_mid=(k3 * jnp.exp(jnp.minimum(b_mid - b3, EXP_CLAMP))).astype(BF16),
            decay=jnp.exp(b_last),
            state=st_ref[hd]))

    for n in range(n_chunks):
        lo, hi = n * B_CHUNK, (n + 1) * B_CHUNK
        for hd in range(heads):
            p = prep[hd]
            vs = slice(hd * B_DV, (hd + 1) * B_DV)
            vn = v_ref[lo:hi, vs]
            attn = lax.dot_general(p["q_mid"][n], p["k_mid"][n], _NT, preferred_element_type=F32)
            attn = jnp.where(tril, attn, 0.0).astype(BF16)
            o = jnp.dot(attn, vn, preferred_element_type=F32)
            o = o + lax.dot_general(p["q_in"][n], p["state"].astype(BF16), _NT,
                                    preferred_element_type=F32)
            upd = lax.dot_general(vn, p["k_out"][n], _TN, preferred_element_type=F32)
            p["state"] = p["decay"][n] * p["state"] + upd
            o = o * lax.rsqrt(jnp.mean(o * o, axis=-1, keepdims=True) + RMS_EPS) * gn
            rn = r_ref[lo:hi, vs].astype(F32)
            gate = rn / (1.0 + jnp.exp(-rn))
            y_ref[lo:hi, vs] = (o * gate).astype(BF16)
    for hd in range(heads):
        st_ref[hd] = prep[hd]["state"]


def _gla(q, k, v, r, a, w2, ba, gn, *, batch, seq, rows, heads):
    t = batch * seq
    nb = seq // rows
    tile = lambda b, h, j: (b * nb + j, h)
    return pl.pallas_call(
        functools.partial(_gla_kernel, rows=rows, heads=heads),
        grid=(batch, B_HEADS // heads, nb),
        in_specs=[pl.BlockSpec((rows, heads * B_DK), tile), pl.BlockSpec((rows, heads * B_DK), tile),
                  pl.BlockSpec((rows, heads * B_DV), tile), pl.BlockSpec((rows, heads * B_DV), tile),
                  pl.BlockSpec((rows, B_GATE_RANK), lambda b, h, j: (b * nb + j, 0)),
                  pl.BlockSpec((B_GATE_RANK, heads * B_DK), lambda b, h, j: (0, h)),
                  pl.BlockSpec((1, heads * B_DK), lambda b, h, j: (0, h)),
                  _const_spec(gn.shape)],
        out_specs=pl.BlockSpec((rows, heads * B_DV), tile),
        out_shape=jax.ShapeDtypeStruct((t, B_V), BF16),
        scratch_shapes=[pltpu.VMEM((heads, B_DV, B_DK), F32)],
        compiler_params=_params("parallel", "parallel", "arbitrary"),
        name="gla",
    )(q, k, v, r, a, w2, ba, gn)


def _tile(n, pref):
    return pref if n % pref == 0 else n


def _forward(x, positions, a_w_in, a_w_o, b_w_in, b_w_a2, b_b_a, b_g_norm, b_w_o,
             ln_mix_g, ln_mix_b, mlp_w_up, mlp_w_down, ln_mlp_g, ln_mlp_b):
    batch, seq, _ = x.shape
    t = batch * seq
    tm = _tile(t, 512)
    tq = _tile(seq, 256)
    rows = _tile(seq, 512)

    h = x.reshape(t, D_MODEL)
    pos = positions.reshape(t, 1).astype(F32)
    inv = ROPE_THETA ** (-jnp.arange(0, ROT_DIM, 2, dtype=F32) / ROT_DIM)
    inv = jnp.concatenate([inv, inv, jnp.zeros((A_HEAD_DIM - ROT_DIM,), F32)])
    inv = jnp.tile(inv, LANES // A_HEAD_DIM).reshape(1, LANES)
    row = lambda p: p.reshape(1, -1)

    for layer in range(DEPTH):
        j = layer // 2
        if layer % 2 == 0:
            w = a_w_in[j].astype(BF16)
            o1, o2, o3, o4 = A_Q, A_Q + A_KV, A_Q + 2 * A_KV, A_Q + 2 * A_KV + A_IQ
            o5 = o4 + IDX_DIM
            wik = jnp.pad(w[:, o4:o5], ((0, 0), (0, LANES - IDX_DIM)))
            wiwt = jnp.pad(w[:, o5:].T, ((0, 16 - IDX_HEADS), (0, 0)))
            q, k, vt, iq, ik, iwt = _dsa_proj(h, pos, inv, w[:, :o1], w[:, o1:o2], w[:, o2:o3].T,
                                              w[:, o3:o4], wik, wiwt, tm=tm, tkv=tq)
            y = _dsa_attn(q, k, vt, iq, ik, iwt, batch=batch, seq=seq, tq=tq)
            w_o = a_w_o[j].astype(BF16)
        else:
            w = b_w_in[j].astype(BF16)
            o1, o2, o3, o4 = B_QK, 2 * B_QK, 2 * B_QK + B_V, 2 * B_QK + 2 * B_V
            wa = jnp.pad(w[:, o4:], ((0, 0), (0, LANES - B_GATE_RANK)))
            q, k, v, r, a = _gla_proj(h, w[:, :o1], w[:, o1:o2], w[:, o2:o3], w[:, o3:o4], wa, tm=tm)
            y = _gla(q, k, v, r, a, b_w_a2[j], row(b_b_a[j]), row(b_g_norm[j]),
                     batch=batch, seq=seq, rows=rows, heads=GLA_HEADS_PER_STEP)
            w_o = b_w_o[j].astype(BF16)
        h = _mix_mlp(y, w_o, h, row(ln_mix_g[layer]), row(ln_mix_b[layer]),
                     mlp_w_up[layer].astype(BF16), mlp_w_down[layer].astype(BF16),
                     row(ln_mlp_g[layer]), row(ln_mlp_b[layer]), tm=tm, tf=1024)
    return h.reshape(batch, seq, D_MODEL)


_forward_jit = jax.jit(_forward)


def kernel(x, positions, a_w_in, a_w_o, b_w_in, b_w_a2, b_b_a, b_g_norm, b_w_o,
           ln_mix_g, ln_mix_b, mlp_w_up, mlp_w_down, ln_mlp_g, ln_mlp_b):
    return _forward_jit(x, positions, a_w_in, a_w_o, b_w_in, b_w_a2, b_b_a, b_g_norm, b_w_o,
                        ln_mix_g, ln_mix_b, mlp_w_up, mlp_w_down, ln_mlp_g, ln_mlp_b)
```

```python
import functools

import jax
import jax.numpy as jnp
from jax import lax
from jax.experimental import pallas as pl
from jax.experimental.pallas import tpu as pltpu

D_MODEL = 1024
DEPTH = 4

A_HEADS = 16
A_KV_HEADS = 4
A_HEAD_DIM = 64
A_REP = A_HEADS // A_KV_HEADS
IDX_HEADS = 8
IDX_DIM = 64
IDX_TOPK_MAX = 256
A_Q = A_HEADS * A_HEAD_DIM
A_KV = A_KV_HEADS * A_HEAD_DIM
A_IQ = IDX_HEADS * IDX_DIM

B_HEADS = 4
B_DK = D_MODEL // 2 // B_HEADS
B_DV = D_MODEL // B_HEADS
B_GATE_RANK = 16
B_GATE_TAU = 16.0
B_CHUNK = 64
B_QK = B_HEADS * B_DK
B_V = B_HEADS * B_DV

D_FF = 4 * D_MODEL

ROPE_THETA = 500000.0
ROT_DIM = A_HEAD_DIM // 4
ROT_HALF = ROT_DIM // 2
LN_EPS = 1e-5
RMS_EPS = 1e-6
DN_ALPHA = (2 * DEPTH) ** 0.25

LANES = 128
BF16_SUBLANES = 16
VMEM_LIMIT_BYTES = 56 * 1024 * 1024
INT32_MIN = -(2 ** 31)
INT32_MAX = 2 ** 31 - 1
ZERO_BAND_MIN = -(2 ** 23)
NEG_BIG = -(2.0 ** 100)
EXP_CLAMP = 80.0
LOG2_E = 1.4426950408889634
PV_ROWS = A_HEAD_DIM + BF16_SUBLANES
SEARCH_VALUE_ROUNDS = 24
SEARCH_ROUNDS_PER_CHECK = 3
FLASH_HEADS = 4
GLA_HEADS_PER_STEP = 4
FOLD_ROWS = 32

BF16 = jnp.bfloat16
F32 = jnp.float32

_NT = (((1,), (1,)), ((), ()))
_TN = (((0,), (0,)), ((), ()))


def _params(*sem):
    return pltpu.CompilerParams(dimension_semantics=sem,
                                vmem_limit_bytes=VMEM_LIMIT_BYTES)


def _const_spec(shape):
    nd = len(shape)
    return pl.BlockSpec(shape, lambda *_: (0,) * nd)


def _float_to_key(x):
    bits = lax.bitcast_convert_type(x, jnp.int32)
    return bits ^ ((bits >> 31) & INT32_MAX)


def _key_to_float(key):
    return lax.bitcast_convert_type(key ^ ((key >> 31) & INT32_MAX), F32)


def _layer_norm(z, g, b):
    mu = jnp.mean(z, axis=-1, keepdims=True)
    zc = z - mu
    var = jnp.mean(zc * zc, axis=-1, keepdims=True)
    return zc * lax.rsqrt(var + LN_EPS) * g + b


def _rot_tables(pos, inv):
    ang = pos * inv
    c = jnp.cos(ang)
    s = jnp.sin(ang)
    lane = lax.broadcasted_iota(jnp.int32, ang.shape, 1) % A_HEAD_DIM
    s_lo = jnp.where(lane < ROT_HALF, -s, 0.0)
    s_hi = jnp.where((lane >= ROT_HALF) & (lane < ROT_DIM), s, 0.0)
    return c, s_lo, s_hi


def _rotate(t, c, s_lo, s_hi):
    w = t.shape[1]
    reps = w // LANES
    if reps > 1:
        c = jnp.concatenate([c] * reps, axis=1)
        s_lo = jnp.concatenate([s_lo] * reps, axis=1)
        s_hi = jnp.concatenate([s_hi] * reps, axis=1)
    up = pltpu.roll(t, w - ROT_HALF, axis=1)
    dn = pltpu.roll(t, ROT_HALF, axis=1)
    return t * c + up * s_lo + dn * s_hi


def _dsa_proj_kernel(h_ref, pos_ref, inv_ref, wq_ref, wk_ref, wvt_ref, wiq_ref, wik_ref, wiwt_ref,
                     q_ref, k_ref, vt_ref, iq_ref, ik_ref, iwt_ref, *, tkv):
    xb = h_ref[...].astype(BF16)
    tm = xb.shape[0]
    c, s_lo, s_hi = _rot_tables(pos_ref[...], inv_ref[...])

    q = jnp.dot(xb, wq_ref[...], preferred_element_type=F32)
    q = _rotate(q, c, s_lo, s_hi) * (A_HEAD_DIM ** -0.5 * LOG2_E)
    for hh in range(A_HEADS):
        q_ref[hh] = q[:, hh * A_HEAD_DIM:(hh + 1) * A_HEAD_DIM].astype(BF16)

    k = jnp.dot(xb, wk_ref[...], preferred_element_type=F32)
    k = _rotate(k, c, s_lo, s_hi)
    vt = lax.dot_general(wvt_ref[...], xb, _NT, preferred_element_type=F32)
    ones_row = jnp.where(lax.broadcasted_iota(jnp.int32, (PV_ROWS - A_HEAD_DIM, tkv), 0) == 0, 1.0, 0.0)
    for g in range(A_KV_HEADS):
        k_ref[g] = k[:, g * A_HEAD_DIM:(g + 1) * A_HEAD_DIM].astype(BF16)
        for j in range(tm // tkv):
            vt_gj = vt[g * A_HEAD_DIM:(g + 1) * A_HEAD_DIM, j * tkv:(j + 1) * tkv]
            vt_ref[g, j] = jnp.concatenate([vt_gj, ones_row], axis=0).astype(BF16)

    iq = jnp.dot(xb, wiq_ref[...], preferred_element_type=F32)
    iq = _rotate(iq, c, s_lo, s_hi) * (IDX_DIM ** -0.5)
    for hh in range(IDX_HEADS):
        iq_ref[hh] = iq[:, hh * IDX_DIM:(hh + 1) * IDX_DIM].astype(BF16)

    x = jnp.dot(xb, wik_ref[...], preferred_element_type=F32)
    first = lax.broadcasted_iota(jnp.int32, x.shape, 1) < IDX_DIM
    x = _rotate(x, jnp.where(first, c, 1.0), jnp.where(first, s_lo, 0.0), jnp.where(first, s_hi, 0.0))
    ik_ref[...] = x[:, :IDX_DIM].astype(BF16)
    iwt = lax.dot_general(wiwt_ref[...], xb, _NT, preferred_element_type=F32)
    iwt_ref[...] = iwt[:IDX_HEADS] * (IDX_HEADS ** -0.5)


def _dsa_proj(h, pos, inv, wq, wk, wvt, wiq, wik, wiwt, *, tm, tkv):
    t = h.shape[0]
    row = lambda i: (i, 0)
    head = lambda i: (0, i, 0)
    return pl.pallas_call(
        functools.partial(_dsa_proj_kernel, tkv=tkv),
        grid=(t // tm,),
        in_specs=[pl.BlockSpec((tm, D_MODEL), row), pl.BlockSpec((tm, 1), row),
                  _const_spec(inv.shape), _const_spec(wq.shape), _const_spec(wk.shape),
                  _const_spec(wvt.shape), _const_spec(wiq.shape), _const_spec(wik.shape),
                  _const_spec(wiwt.shape)],
        out_specs=[pl.BlockSpec((A_HEADS, tm, A_HEAD_DIM), head),
                   pl.BlockSpec((A_KV_HEADS, tm, A_HEAD_DIM), head),
                   pl.BlockSpec((A_KV_HEADS, tm // tkv, PV_ROWS, tkv), lambda i: (0, i, 0, 0)),
                   pl.BlockSpec((IDX_HEADS, tm, IDX_DIM), head),
                   pl.BlockSpec((tm, IDX_DIM), row),
                   pl.BlockSpec((IDX_HEADS, tm), lambda i: (0, i))],
        out_shape=[jax.ShapeDtypeStruct((A_HEADS, t, A_HEAD_DIM), BF16),
                   jax.ShapeDtypeStruct((A_KV_HEADS, t, A_HEAD_DIM), BF16),
                   jax.ShapeDtypeStruct((A_KV_HEADS, t // tkv, PV_ROWS, tkv), BF16),
                   jax.ShapeDtypeStruct((IDX_HEADS, t, IDX_DIM), BF16),
                   jax.ShapeDtypeStruct((t, IDX_DIM), BF16),
                   jax.ShapeDtypeStruct((IDX_HEADS, t), F32)],
        compiler_params=_params("parallel"),
        name="dsa_proj",
    )(h, pos, inv, wq, wk, wvt, wiq, wik, wiwt)


def _dsa_attn_kernel(q_ref, iq_ref, iwt_ref, k_ref, vt_ref, ik_ref, o_ref,
                     slab, m_sc, acc_sc, cut_sc, ot_sc, *, tq, n_qt, top_k):
    i = pl.program_id(1)
    n_kt = i + 1
    q_pos = i * tq + lax.broadcasted_iota(jnp.int32, (1, tq), 1)
    key_off = lax.broadcasted_iota(jnp.int32, (tq, tq), 0)

    iw = iwt_ref[...]

    def fold(x, op):
        return op(x.reshape(-1, FOLD_ROWS, tq), axis=0)

    n_pairs = (n_kt + 1) // 2

    def score_pair(j, carry):
        mx, mn, c0, c1 = carry
        for kt in (2 * j, 2 * j + 1):
            ik_t = ik_ref[pl.ds(pl.multiple_of(jnp.minimum(kt, n_qt - 1) * tq, tq), tq), :]
            sc = jnp.zeros((tq, tq), F32)
            for hh in range(IDX_HEADS):
                s = lax.dot_general(ik_t, iq_ref[hh], _NT, preferred_element_type=F32)
                sc = sc + iw[hh:hh + 1, :] * jnp.maximum(s, 0.0)
            key = _float_to_key(sc)
            idx = kt * tq + key_off
            key = jnp.where(key == 0, -1 - idx, key)
            causal = idx <= q_pos
            key = jnp.where(causal, key, INT32_MIN)
            slab[kt] = key
            mx = jnp.maximum(mx, fold(key, jnp.max))
            mn = jnp.minimum(mn, fold(jnp.where(causal, key, INT32_MAX), jnp.min))
            c0 = c0 + fold(jnp.where(key >= ZERO_BAND_MIN, 1, 0), jnp.sum)
            c1 = c1 + fold(jnp.where(key >= 1, 1, 0), jnp.sum)
        return mx, mn, c0, c1

    part = (FOLD_ROWS, tq)
    init = (jnp.full(part, INT32_MIN, jnp.int32), jnp.full(part, INT32_MAX, jnp.int32),
            jnp.zeros(part, jnp.int32), jnp.zeros(part, jnp.int32))
    mx, mn, c0, c1 = lax.fori_loop(0, n_pairs, score_pair, init)
    key_max = jnp.max(mx, axis=0, keepdims=True)
    key_min = jnp.min(mn, axis=0, keepdims=True)
    cnt_nonneg = jnp.sum(c0, axis=0, keepdims=True)
    cnt_pos = jnp.sum(c1, axis=0, keepdims=True)

    k_row = jnp.minimum(top_k, q_pos + 1)

    def count(pred):
        def body(j, acc):
            hit = jnp.where(pred(slab[pl.ds(2 * j, 2)]), 1, 0)
            return acc + fold(hit, jnp.sum)
        acc = lax.fori_loop(0, n_pairs, body, jnp.zeros(part, jnp.int32))
        return jnp.sum(acc, axis=0, keepdims=True)

    pos_case = cnt_pos >= k_row
    zero_case = jnp.logical_not(pos_case) & (cnt_nonneg >= k_row)
    lo0 = jnp.where(pos_case, 1, jnp.where(zero_case, -1 - n_kt * tq, key_min))
    clo0 = jnp.where(pos_case, cnt_pos, jnp.where(zero_case, cnt_nonneg, q_pos + 1))
    hi0 = jnp.where(pos_case, key_max + 1, jnp.where(zero_case, 0, ZERO_BAND_MIN))
    chi0 = jnp.where(pos_case, 0, jnp.where(zero_case, cnt_pos, cnt_nonneg))

    def n_active(lo, hi, clo):
        return jnp.max(jnp.where((clo != k_row) & (hi - 1 > lo), 1, 0))

    def search_cond(carry):
        it, nact = carry[0], carry[1]
        return (nact > 0) & (it < SEARCH_VALUE_ROUNDS + 33)

    def search_round(it, lo, hi, clo, chi, bisect):
        active = (clo != k_row) & (hi - 1 > lo)
        lov = _key_to_float(lo)
        hiv = _key_to_float(hi)
        log_clo = jnp.log(clo.astype(F32) + 0.5)
        frac_interp = (log_clo - log_k) / (log_clo - jnp.log(chi.astype(F32) + 0.5))
        frac = jnp.where(bisect, 0.5, frac_interp)
        cand_value = _float_to_key(lov + (hiv - lov) * frac)
        cand_key = lo + ((hi - lo) >> 1)
        cand = jnp.where((it >= SEARCH_VALUE_ROUNDS) | zero_case, cand_key, cand_value)
        cand = jnp.minimum(jnp.maximum(cand, lo + 1), hi - 1)
        cand = jnp.where(active, cand, lo)
        cnt = count(lambda ky: ky >= cand)
        ok = cnt >= k_row
        return (jnp.where(ok, cand, lo), jnp.where(ok, hi, cand),
                jnp.where(ok, cnt, clo), jnp.where(ok, chi, cnt))

    log_k = jnp.log(k_row.astype(F32) + 0.5)

    def search_body(carry):
        it, _, lo, hi, clo, chi = carry
        group = it // SEARCH_ROUNDS_PER_CHECK
        for r in range(SEARCH_ROUNDS_PER_CHECK):
            if r == SEARCH_ROUNDS_PER_CHECK - 1:
                bisect = (group > 0) & (group % 2 == 0)
            else:
                bisect = group == 0
            lo, hi, clo, chi = search_round(it, lo, hi, clo, chi, bisect)
        return it + SEARCH_ROUNDS_PER_CHECK, n_active(lo, hi, clo), lo, hi, clo, chi

    _, _, thr, _, nge, _ = lax.while_loop(
        search_cond, search_body, (jnp.int32(0), n_active(lo0, hi0, clo0), lo0, hi0, clo0, chi0))

    cut_sc[...] = jnp.full((1, tq), n_kt * tq, jnp.int32)
    has_ties = jnp.max(nge - k_row) > 0

    @pl.when(has_ties)
    def _():
        n_gt = count(lambda ky: ky > thr)
        need = (k_row - n_gt).astype(F32)
        tri = jnp.where(lax.broadcasted_iota(jnp.int32, (tq, tq), 1) <= key_off, 1.0, 0.0).astype(BF16)

        def tie_tile(kt, carry):
            seen, best = carry
            eq = slab[kt] == thr
            rank = jnp.dot(tri, jnp.where(eq, 1.0, 0.0).astype(BF16),
                           preferred_element_type=F32) + seen
            hit = eq & (rank == need)
            best = jnp.maximum(best, fold(jnp.where(hit, kt * tq + key_off, -1), jnp.max))
            return rank[tq - 1:tq, :], best

        _, best = lax.fori_loop(0, n_kt, tie_tile,
                                (jnp.zeros((1, tq), F32), jnp.full(part, -1, jnp.int32)))
        cut_sc[...] = jnp.max(best, axis=0, keepdims=True)

    cut = cut_sc[...]

    m_sc[...] = jnp.full(m_sc.shape, NEG_BIG, F32)
    acc_sc[...] = jnp.zeros(acc_sc.shape, F32)

    n_steps = A_HEADS // FLASH_HEADS

    def mask_bias(kt):
        keys = slab[kt]
        sel = (keys > thr) | ((keys == thr) & ((kt * tq + key_off) <= cut))
        return jnp.where(sel, 0.0, NEG_BIG).astype(BF16)

    def logits(kt, st, bias):
        qs = q_ref[st * FLASH_HEADS:(st + 1) * FLASH_HEADS].reshape(FLASH_HEADS * tq, A_HEAD_DIM)
        kg = k_ref[st * FLASH_HEADS // A_REP, pl.ds(pl.multiple_of(kt * tq, tq), tq), :]
        s = lax.dot_general(kg, qs, _NT, preferred_element_type=F32).astype(BF16)
        return s + jnp.concatenate([bias] * FLASH_HEADS, axis=1)

    def attn_tile(kt, carry):
        s_next, bias = carry
        for st in range(n_steps):
            s = s_next
            if st + 1 < n_steps:
                s_next = logits(kt, st + 1, bias)
            else:
                kt_next = jnp.minimum(kt + 1, n_kt - 1)
                bias = mask_bias(kt_next)
                s_next = logits(kt_next, 0, bias)
            m_prev = m_sc[st]
            m_new = jnp.maximum(m_prev, jnp.max(s, axis=0, keepdims=True).astype(F32))
            alpha = jnp.exp2(m_prev - m_new)
            p = jnp.exp2(s - m_new.astype(BF16))
            acc_sc[st] = alpha * acc_sc[st] + jnp.dot(vt_ref[st * FLASH_HEADS // A_REP, kt], p,
                                                      preferred_element_type=F32)
            m_sc[st] = m_new
        return s_next, bias

    def attn_pair(j, carry):
        return attn_tile(2 * j + 1, attn_tile(2 * j, carry))

    bias0 = mask_bias(0)
    s_last, bias_last = lax.fori_loop(0, n_kt // 2, attn_pair, (logits(0, 0, bias0), bias0))

    @pl.when(n_kt % 2 == 1)
    def _():
        attn_tile(n_kt - 1, (s_last, bias_last))

    for st in range(n_steps):
        acc = acc_sc[st]
        og = acc[:A_HEAD_DIM] / acc[A_HEAD_DIM:A_HEAD_DIM + 1]
        for r in range(FLASH_HEADS):
            hh = st * FLASH_HEADS + r
            ot_sc[hh * A_HEAD_DIM:(hh + 1) * A_HEAD_DIM, :] = og[:, r * tq:(r + 1) * tq]
    o_ref[...] = ot_sc[...].T.astype(BF16)


def _dsa_attn(q, k, vt, iq, ik, iwt, *, batch, seq, tq):
    t = batch * seq
    nq = seq // tq
    top_k = min(IDX_TOPK_MAX, seq // 4)
    qtile = lambda b, i: (0, b * nq + i, 0)
    whole = lambda b, i: (0, b, 0)
    once = pl.Buffered(1)
    return pl.pallas_call(
        functools.partial(_dsa_attn_kernel, tq=tq, n_qt=nq, top_k=top_k),
        grid=(batch, nq),
        in_specs=[pl.BlockSpec((A_HEADS, tq, A_HEAD_DIM), qtile),
                  pl.BlockSpec((IDX_HEADS, tq, IDX_DIM), qtile),
                  pl.BlockSpec((IDX_HEADS, tq), lambda b, i: (0, b * nq + i)),
                  pl.BlockSpec((A_KV_HEADS, seq, A_HEAD_DIM), whole, pipeline_mode=once),
                  pl.BlockSpec((A_KV_HEADS, nq, PV_ROWS, tq), lambda b, i: (0, b, 0, 0),
                               pipeline_mode=once),
                  pl.BlockSpec((seq, IDX_DIM), lambda b, i: (b, 0), pipeline_mode=once)],
        out_specs=pl.BlockSpec((tq, A_Q), lambda b, i: (b * nq + i, 0)),
        out_shape=jax.ShapeDtypeStruct((t, A_Q), BF16),
        scratch_shapes=[pltpu.VMEM((nq + nq % 2, tq, tq), jnp.int32),
                        pltpu.VMEM((A_HEADS // FLASH_HEADS, 1, FLASH_HEADS * tq), F32),
                        pltpu.VMEM((A_HEADS // FLASH_HEADS, PV_ROWS, FLASH_HEADS * tq), F32),
                        pltpu.VMEM((1, tq), jnp.int32),
                        pltpu.VMEM((A_Q, tq), F32)],
        compiler_params=_params("parallel", "arbitrary"),
        name="dsa_attn",
    )(q, iq, iwt, k, vt, ik)


def _mix_mlp_kernel(y_ref, wo_ref, h_ref, g1_ref, b1_ref, wu_ref, wd_ref, g2_ref, b2_ref, o_ref, *, tf):
    mix = jnp.dot(y_ref[...], wo_ref[...], preferred_element_type=F32)
    h = _layer_norm(DN_ALPHA * h_ref[...] + mix, g1_ref[...], b1_ref[...])
    xb = h.astype(BF16)
    acc = jnp.zeros(h.shape, F32)
    for c in range(D_FF // tf):
        u = jnp.dot(xb, wu_ref[:, c * tf:(c + 1) * tf], preferred_element_type=F32)
        u = jnp.square(jnp.maximum(u, 0.0)).astype(BF16)
        acc = acc + jnp.dot(u, wd_ref[c * tf:(c + 1) * tf, :], preferred_element_type=F32)
    o_ref[...] = _layer_norm(DN_ALPHA * h + acc, g2_ref[...], b2_ref[...])


def _mix_mlp(y, wo, h, g1, b1, wu, wd, g2, b2, *, tm, tf):
    t, kdim = y.shape
    row = lambda i: (i, 0)
    once = pl.Buffered(1)
    resident = lambda w: pl.BlockSpec(w.shape, lambda i: (0, 0), pipeline_mode=once)
    return pl.pallas_call(
        functools.partial(_mix_mlp_kernel, tf=tf),
        grid=(t // tm,),
        in_specs=[pl.BlockSpec((tm, kdim), row), resident(wo), pl.BlockSpec((tm, D_MODEL), row),
                  _const_spec(g1.shape), _const_spec(b1.shape), resident(wu), resident(wd),
                  _const_spec(g2.shape), _const_spec(b2.shape)],
        out_specs=pl.BlockSpec((tm, D_MODEL), row),
        out_shape=jax.ShapeDtypeStruct((t, D_MODEL), F32),
        compiler_params=_params("parallel"),
        name="mix_mlp",
    )(y, wo, h, g1, b1, wu, wd, g2, b2)


def _gla_proj_kernel(h_ref, wq_ref, wk_ref, wv_ref, wr_ref, wa_ref,
                     q_ref, k_ref, v_ref, r_ref, a_ref):
    xb = h_ref[...].astype(BF16)
    q = jnp.dot(xb, wq_ref[...], preferred_element_type=F32)
    q_ref[...] = (q * (B_DK ** -0.5)).astype(BF16)
    k_ref[...] = jnp.dot(xb, wk_ref[...], preferred_element_type=F32).astype(BF16)
    v_ref[...] = jnp.dot(xb, wv_ref[...], preferred_element_type=F32).astype(BF16)
    r_ref[...] = jnp.dot(xb, wr_ref[...], preferred_element_type=F32).astype(BF16)
    a = jnp.dot(xb, wa_ref[...], preferred_element_type=F32)
    a_ref[...] = a[:, :B_GATE_RANK]


def _gla_proj(h, wq, wk, wv, wr, wa, *, tm):
    t = h.shape[0]
    row = lambda i: (i, 0)
    return pl.pallas_call(
        _gla_proj_kernel,
        grid=(t // tm,),
        in_specs=[pl.BlockSpec((tm, D_MODEL), row), _const_spec(wq.shape), _const_spec(wk.shape),
                  _const_spec(wv.shape), _const_spec(wr.shape), _const_spec(wa.shape)],
        out_specs=[pl.BlockSpec((tm, B_QK), row), pl.BlockSpec((tm, B_QK), row),
                   pl.BlockSpec((tm, B_V), row), pl.BlockSpec((tm, B_V), row),
                   pl.BlockSpec((tm, B_GATE_RANK), row)],
        out_shape=[jax.ShapeDtypeStruct((t, B_QK), BF16), jax.ShapeDtypeStruct((t, B_QK), BF16),
                   jax.ShapeDtypeStruct((t, B_V), BF16), jax.ShapeDtypeStruct((t, B_V), BF16),
                   jax.ShapeDtypeStruct((t, B_GATE_RANK), F32)],
        compiler_params=_params("parallel"),
        name="gla_proj",
    )(h, wq, wk, wv, wr, wa)


def _gla_kernel(q_ref, k_ref, v_ref, r_ref, a_ref, w2_ref, ba_ref, gn_ref, y_ref, st_ref, *, rows, heads):
    n_chunks = rows // B_CHUNK

    @pl.when(pl.program_id(2) == 0)
    def _():
        st_ref[...] = jnp.zeros(st_ref.shape, F32)

    x = jnp.dot(a_ref[...], w2_ref[...], preferred_element_type=F32,
                precision=lax.Precision.HIGHEST) + ba_ref[...]
    log_a = (jnp.minimum(x, 0.0) - jnp.log1p(jnp.exp(-jnp.abs(x)))) / B_GATE_TAU

    row_in_chunk = lax.broadcasted_iota(jnp.int32, log_a.shape, 0) % B_CHUNK
    b = log_a
    step = 1
    while step < B_CHUNK:
        b = b + jnp.where(row_in_chunk >= step, pltpu.roll(b, step, axis=0), 0.0)
        step *= 2

    tril = (lax.broadcasted_iota(jnp.int32, (B_CHUNK, B_CHUNK), 1)
            <= lax.broadcasted_iota(jnp.int32, (B_CHUNK, B_CHUNK), 0))
    gn = gn_ref[...]

    prep = []
    for hd in range(heads):
        ks = slice(hd * B_DK, (hd + 1) * B_DK)
        b3 = b[:, ks].reshape(n_chunks, B_CHUNK, B_DK)
        b_last = b3[:, B_CHUNK - 1:B_CHUNK, :]
        b_mid = b3[:, B_CHUNK // 2 - 1:B_CHUNK // 2, :]
        q3 = q_ref[:, ks].astype(F32).reshape(n_chunks, B_CHUNK, B_DK)
        k3 = k_ref[:, ks].astype(F32).reshape(n_chunks, B_CHUNK, B_DK)
        prep.append(dict(
            q_in=(q3 * jnp.exp(b3)).astype(BF16),
            k_out=(k3 * jnp.exp(b_last - b3)).astype(BF16),
            q_mid=(q3 * jnp.exp(jnp.minimum(b3 - b_mid, EXP_CLAMP))).astype(BF16),
            k_mid=(k3 * jnp.exp(jnp.minimum(b_mid - b3, EXP_CLAMP))).astype(BF16),
            decay=jnp.exp(b_last),
            state=st_ref[hd]))

    for n in range(n_chunks):
        lo, hi = n * B_CHUNK, (n + 1) * B_CHUNK
        for hd in range(heads):
            p = prep[hd]
            vs = slice(hd * B_DV, (hd + 1) * B_DV)
            vn = v_ref[lo:hi, vs]
            attn = lax.dot_general(p["q_mid"][n], p["k_mid"][n], _NT, preferred_element_type=F32)
            attn = jnp.where(tril, attn, 0.0).astype(BF16)
            o = jnp.dot(attn, vn, preferred_element_type=F32)
            o = o + lax.dot_general(p["q_in"][n], p["state"].astype(BF16), _NT,
                                    preferred_element_type=F32)
            upd = lax.dot_general(vn, p["k_out"][n], _TN, preferred_element_type=F32)
            p["state"] = p["decay"][n] * p["state"] + upd
            o = o * lax.rsqrt(jnp.mean(o * o, axis=-1, keepdims=True) + RMS_EPS) * gn
            rn = r_ref[lo:hi, vs].astype(F32)
            gate = rn / (1.0 + jnp.exp(-rn))
            y_ref[lo:hi, vs] = (o * gate).astype(BF16)
    for hd in range(heads):
        st_ref[hd] = prep[hd]["state"]


def _gla(q, k, v, r, a, w2, ba, gn, *, batch, seq, rows, heads):
    t = batch * seq
    nb = seq // rows
    tile = lambda b, h, j: (b * nb + j, h)
    return pl.pallas_call(
        functools.partial(_gla_kernel, rows=rows, heads=heads),
        grid=(batch, B_HEADS // heads, nb),
        in_specs=[pl.BlockSpec((rows, heads * B_DK), tile), pl.BlockSpec((rows, heads * B_DK), tile),
                  pl.BlockSpec((rows, heads * B_DV), tile), pl.BlockSpec((rows, heads * B_DV), tile),
                  pl.BlockSpec((rows, B_GATE_RANK), lambda b, h, j: (b * nb + j, 0)),
                  pl.BlockSpec((B_GATE_RANK, heads * B_DK), lambda b, h, j: (0, h)),
                  pl.BlockSpec((1, heads * B_DK), lambda b, h, j: (0, h)),
                  _const_spec(gn.shape)],
        out_specs=pl.BlockSpec((rows, heads * B_DV), tile),
        out_shape=jax.ShapeDtypeStruct((t, B_V), BF16),
        scratch_shapes=[pltpu.VMEM((heads, B_DV, B_DK), F32)],
        compiler_params=_params("parallel", "parallel", "arbitrary"),
        name="gla",
    )(q, k, v, r, a, w2, ba, gn)


def _tile(n, pref):
    return pref if n % pref == 0 else n


def _forward(x, positions, a_w_in, a_w_o, b_w_in, b_w_a2, b_b_a, b_g_norm, b_w_o,
             ln_mix_g, ln_mix_b, mlp_w_up, mlp_w_down, ln_mlp_g, ln_mlp_b):
    batch, seq, _ = x.shape
    t = batch * seq
    tm = _tile(t, 512)
    tq = _tile(seq, 256)
    rows = _tile(seq, 512)

    h = x.reshape(t, D_MODEL)
    pos = positions.reshape(t, 1).astype(F32)
    inv = ROPE_THETA ** (-jnp.arange(0, ROT_DIM, 2, dtype=F32) / ROT_DIM)
    inv = jnp.concatenate([inv, inv, jnp.zeros((A_HEAD_DIM - ROT_DIM,), F32)])
    inv = jnp.tile(inv, LANES // A_HEAD_DIM).reshape(1, LANES)
    row = lambda p: p.reshape(1, -1)

    for layer in range(DEPTH):
        j = layer // 2
        if layer % 2 == 0:
            w = a_w_in[j].astype(BF16)
            o1, o2, o3, o4 = A_Q, A_Q + A_KV, A_Q + 2 * A_KV, A_Q + 2 * A_KV + A_IQ
            o5 = o4 + IDX_DIM
            wik = jnp.pad(w[:, o4:o5], ((0, 0), (0, LANES - IDX_DIM)))
            wiwt = jnp.pad(w[:, o5:].T, ((0, BF16_SUBLANES - IDX_HEADS), (0, 0)))
            q, k, vt, iq, ik, iwt = _dsa_proj(h, pos, inv, w[:, :o1], w[:, o1:o2], w[:, o2:o3].T,
                                              w[:, o3:o4], wik, wiwt, tm=tm, tkv=tq)
            y = _dsa_attn(q, k, vt, iq, ik, iwt, batch=batch, seq=seq, tq=tq)
            w_o = a_w_o[j].astype(BF16)
        else:
            w = b_w_in[j].astype(BF16)
            o1, o2, o3, o4 = B_QK, 2 * B_QK, 2 * B_QK + B_V, 2 * B_QK + 2 * B_V
            wa = jnp.pad(w[:, o4:], ((0, 0), (0, LANES - B_GATE_RANK)))
            q, k, v, r, a = _gla_proj(h, w[:, :o1], w[:, o1:o2], w[:, o2:o3], w[:, o3:o4], wa, tm=tm)
            y = _gla(q, k, v, r, a, b_w_a2[j], row(b_b_a[j]), row(b_g_norm[j]),
                     batch=batch, seq=seq, rows=rows, heads=GLA_HEADS_PER_STEP)
            w_o = b_w_o[j].astype(BF16)
        h = _mix_mlp(y, w_o, h, row(ln_mix_g[layer]), row(ln_mix_b[layer]),
                     mlp_w_up[layer].astype(BF16), mlp_w_down[layer].astype(BF16),
                     row(ln_mlp_g[layer]), row(ln_mlp_b[layer]), tm=tm, tf=1024)
    return h.reshape(batch, seq, D_MODEL)


_forward_jit = jax.jit(_forward)


def kernel(x, positions, a_w_in, a_w_o, b_w_in, b_w_a2, b_b_a, b_g_norm, b_w_o,
           ln_mix_g, ln_mix_b, mlp_w_up, mlp_w_down, ln_mlp_g, ln_mlp_b):
    return _forward_jit(x, positions, a_w_in, a_w_o, b_w_in, b_w_a2, b_b_a, b_g_norm, b_w_o,
                        ln_mix_g, ln_mix_b, mlp_w_up, mlp_w_down, ln_mlp_g, ln_mlp_b)
```

```python
import functools

import jax
import jax.numpy as jnp
from jax import lax
from jax.experimental import pallas as pl
from jax.experimental.pallas import tpu as pltpu

D_MODEL = 1024
DEPTH = 4

A_HEADS = 16
A_KV_HEADS = 4
A_HEAD_DIM = 64
A_REP = A_HEADS // A_KV_HEADS
IDX_HEADS = 8
IDX_DIM = 64
IDX_TOPK_MAX = 256
A_Q = A_HEADS * A_HEAD_DIM
A_KV = A_KV_HEADS * A_HEAD_DIM
A_IQ = IDX_HEADS * IDX_DIM

B_HEADS = 4
B_DK = D_MODEL // 2 // B_HEADS
B_DV = D_MODEL // B_HEADS
B_GATE_RANK = 16
B_GATE_TAU = 16.0
B_CHUNK = 64
B_QK = B_HEADS * B_DK
B_V = B_HEADS * B_DV

D_FF = 4 * D_MODEL

ROPE_THETA = 500000.0
ROT_DIM = A_HEAD_DIM // 4
ROT_HALF = ROT_DIM // 2
LN_EPS = 1e-5
RMS_EPS = 1e-6
DN_ALPHA = (2 * DEPTH) ** 0.25

LANES = 128
BF16_SUBLANES = 16
VMEM_LIMIT_BYTES = 56 * 1024 * 1024
INT32_MIN = -(2 ** 31)
INT32_MAX = 2 ** 31 - 1
ZERO_BAND_MIN = -(2 ** 23)
NEG_BIG = -(2.0 ** 100)
EXP_CLAMP = 80.0
LOG2_E = 1.4426950408889634
PV_ROWS = A_HEAD_DIM + BF16_SUBLANES
SEARCH_VALUE_ROUNDS = 24
SEARCH_ROUNDS_PER_CHECK = 3
FLASH_HEADS = 4
GLA_HEADS_PER_STEP = 4
FOLD_ROWS = 32

BF16 = jnp.bfloat16
F32 = jnp.float32

_NT = (((1,), (1,)), ((), ()))
_TN = (((0,), (0,)), ((), ()))


def _params(*sem):
    return pltpu.CompilerParams(dimension_semantics=sem,
                                vmem_limit_bytes=VMEM_LIMIT_BYTES)


def _const_spec(shape):
    nd = len(shape)
    return pl.BlockSpec(shape, lambda *_: (0,) * nd)


def _float_to_key(x):
    bits = lax.bitcast_convert_type(x, jnp.int32)
    return bits ^ ((bits >> 31) & INT32_MAX)


def _key_to_float(key):
    return lax.bitcast_convert_type(key ^ ((key >> 31) & INT32_MAX), F32)


def _layer_norm(z, g, b):
    mu = jnp.mean(z, axis=-1, keepdims=True)
    zc = z - mu
    var = jnp.mean(zc * zc, axis=-1, keepdims=True)
    return zc * lax.rsqrt(var + LN_EPS) * g + b


def _rot_tables(pos, inv):
    ang = pos * inv
    c = jnp.cos(ang)
    s = jnp.sin(ang)
    lane = lax.broadcasted_iota(jnp.int32, ang.shape, 1) % A_HEAD_DIM
    s_lo = jnp.where(lane < ROT_HALF, -s, 0.0)
    s_hi = jnp.where((lane >= ROT_HALF) & (lane < ROT_DIM), s, 0.0)
    return c, s_lo, s_hi


def _rotate(t, c, s_lo, s_hi):
    w = t.shape[1]
    reps = w // LANES
    if reps > 1:
        c = jnp.concatenate([c] * reps, axis=1)
        s_lo = jnp.concatenate([s_lo] * reps, axis=1)
        s_hi = jnp.concatenate([s_hi] * reps, axis=1)
    up = pltpu.roll(t, w - ROT_HALF, axis=1)
    dn = pltpu.roll(t, ROT_HALF, axis=1)
    return t * c + up * s_lo + dn * s_hi


def _dsa_proj_kernel(h_ref, pos_ref, inv_ref, wq_ref, wk_ref, wvt_ref, wiq_ref, wik_ref, wiwt_ref,
                     q_ref, k_ref, vt_ref, iq_ref, ik_ref, iwt_ref, *, tkv):
    xb = h_ref[...].astype(BF16)
    tm = xb.shape[0]
    c, s_lo, s_hi = _rot_tables(pos_ref[...], inv_ref[...])

    q = jnp.dot(xb, wq_ref[...], preferred_element_type=F32)
    q = _rotate(q, c, s_lo, s_hi) * (A_HEAD_DIM ** -0.5 * LOG2_E)
    for hh in range(A_HEADS):
        q_ref[hh] = q[:, hh * A_HEAD_DIM:(hh + 1) * A_HEAD_DIM].astype(BF16)

    k = jnp.dot(xb, wk_ref[...], preferred_element_type=F32)
    k = _rotate(k, c, s_lo, s_hi)
    vt = lax.dot_general(wvt_ref[...], xb, _NT, preferred_element_type=F32)
    ones_row = jnp.where(lax.broadcasted_iota(jnp.int32, (PV_ROWS - A_HEAD_DIM, tkv), 0) == 0, 1.0, 0.0)
    for g in range(A_KV_HEADS):
        k_ref[g] = k[:, g * A_HEAD_DIM:(g + 1) * A_HEAD_DIM].astype(BF16)
        for j in range(tm // tkv):
            vt_gj = vt[g * A_HEAD_DIM:(g + 1) * A_HEAD_DIM, j * tkv:(j + 1) * tkv]
            vt_ref[g, j] = jnp.concatenate([vt_gj, ones_row], axis=0).astype(BF16)

    iq = jnp.dot(xb, wiq_ref[...], preferred_element_type=F32)
    iq = _rotate(iq, c, s_lo, s_hi) * (IDX_DIM ** -0.5)
    for hh in range(IDX_HEADS):
        iq_ref[hh] = iq[:, hh * IDX_DIM:(hh + 1) * IDX_DIM].astype(BF16)

    x = jnp.dot(xb, wik_ref[...], preferred_element_type=F32)
    first = lax.broadcasted_iota(jnp.int32, x.shape, 1) < IDX_DIM
    x = _rotate(x, jnp.where(first, c, 1.0), jnp.where(first, s_lo, 0.0), jnp.where(first, s_hi, 0.0))
    ik_ref[...] = x[:, :IDX_DIM].astype(BF16)
    iwt = lax.dot_general(wiwt_ref[...], xb, _NT, preferred_element_type=F32)
    iwt_ref[...] = iwt[:IDX_HEADS] * (IDX_HEADS ** -0.5)


def _dsa_proj(h, pos, inv, wq, wk, wvt, wiq, wik, wiwt, *, tm, tkv):
    t = h.shape[0]
    row = lambda i: (i, 0)
    head = lambda i: (0, i, 0)
    return pl.pallas_call(
        functools.partial(_dsa_proj_kernel, tkv=tkv),
        grid=(t // tm,),
        in_specs=[pl.BlockSpec((tm, D_MODEL), row), pl.BlockSpec((tm, 1), row),
                  _const_spec(inv.shape), _const_spec(wq.shape), _const_spec(wk.shape),
                  _const_spec(wvt.shape), _const_spec(wiq.shape), _const_spec(wik.shape),
                  _const_spec(wiwt.shape)],
        out_specs=[pl.BlockSpec((A_HEADS, tm, A_HEAD_DIM), head),
                   pl.BlockSpec((A_KV_HEADS, tm, A_HEAD_DIM), head),
                   pl.BlockSpec((A_KV_HEADS, tm // tkv, PV_ROWS, tkv), lambda i: (0, i, 0, 0)),
                   pl.BlockSpec((IDX_HEADS, tm, IDX_DIM), head),
                   pl.BlockSpec((tm, IDX_DIM), row),
                   pl.BlockSpec((IDX_HEADS, tm), lambda i: (0, i))],
        out_shape=[jax.ShapeDtypeStruct((A_HEADS, t, A_HEAD_DIM), BF16),
                   jax.ShapeDtypeStruct((A_KV_HEADS, t, A_HEAD_DIM), BF16),
                   jax.ShapeDtypeStruct((A_KV_HEADS, t // tkv, PV_ROWS, tkv), BF16),
                   jax.ShapeDtypeStruct((IDX_HEADS, t, IDX_DIM), BF16),
                   jax.ShapeDtypeStruct((t, IDX_DIM), BF16),
                   jax.ShapeDtypeStruct((IDX_HEADS, t), F32)],
        compiler_params=_params("parallel"),
        name="dsa_proj",
    )(h, pos, inv, wq, wk, wvt, wiq, wik, wiwt)


def _dsa_attn_kernel(q_ref, iq_ref, iwt_ref, k_ref, vt_ref, ik_ref, o_ref,
                     slab, m_sc, acc_sc, cut_sc, ot_sc, *, tq, n_qt, top_k):
    i = pl.program_id(1)
    n_kt = i + 1
    q_pos = i * tq + lax.broadcasted_iota(jnp.int32, (1, tq), 1)
    key_off = lax.broadcasted_iota(jnp.int32, (tq, tq), 0)

    iw = iwt_ref[...]

    def fold(x, op):
        return op(x.reshape(-1, FOLD_ROWS, tq), axis=0)

    n_pairs = (n_kt + 1) // 2

    def score_pair(j, carry):
        mx, mn, c0, c1 = carry
        for kt in (2 * j, 2 * j + 1):
            ik_t = ik_ref[pl.ds(pl.multiple_of(jnp.minimum(kt, n_qt - 1) * tq, tq), tq), :]
            sc = jnp.zeros((tq, tq), F32)
            for hh in range(IDX_HEADS):
                s = lax.dot_general(ik_t, iq_ref[hh], _NT, preferred_element_type=F32)
                sc = sc + iw[hh:hh + 1, :] * jnp.maximum(s, 0.0)
            key = _float_to_key(sc)
            idx = kt * tq + key_off
            key = jnp.where(key == 0, -1 - idx, key)
            causal = idx <= q_pos
            key = jnp.where(causal, key, INT32_MIN)
            slab[kt] = key
            mx = jnp.maximum(mx, fold(key, jnp.max))
            mn = jnp.minimum(mn, fold(jnp.where(causal, key, INT32_MAX), jnp.min))
            c0 = c0 + fold(jnp.where(key >= ZERO_BAND_MIN, 1, 0), jnp.sum)
            c1 = c1 + fold(jnp.where(key >= 1, 1, 0), jnp.sum)
        return mx, mn, c0, c1

    part = (FOLD_ROWS, tq)
    init = (jnp.full(part, INT32_MIN, jnp.int32), jnp.full(part, INT32_MAX, jnp.int32),
            jnp.zeros(part, jnp.int32), jnp.zeros(part, jnp.int32))
    mx, mn, c0, c1 = lax.fori_loop(0, n_pairs, score_pair, init)
    key_max = jnp.max(mx, axis=0, keepdims=True)
    key_min = jnp.min(mn, axis=0, keepdims=True)
    cnt_nonneg = jnp.sum(c0, axis=0, keepdims=True)
    cnt_pos = jnp.sum(c1, axis=0, keepdims=True)

    k_row = jnp.minimum(top_k, q_pos + 1)

    def count(pred):
        def body(j, acc):
            hit = jnp.where(pred(slab[pl.ds(2 * j, 2)]), 1, 0)
            return acc + fold(hit, jnp.sum)
        acc = lax.fori_loop(0, n_pairs, body, jnp.zeros(part, jnp.int32))
        return jnp.sum(acc, axis=0, keepdims=True)

    pos_case = cnt_pos >= k_row
    zero_case = jnp.logical_not(pos_case) & (cnt_nonneg >= k_row)
    lo0 = jnp.where(pos_case, 1, jnp.where(zero_case, -1 - n_kt * tq, key_min))
    clo0 = jnp.where(pos_case, cnt_pos, jnp.where(zero_case, cnt_nonneg, q_pos + 1))
    hi0 = jnp.where(pos_case, key_max + 1, jnp.where(zero_case, 0, ZERO_BAND_MIN))
    chi0 = jnp.where(pos_case, 0, jnp.where(zero_case, cnt_pos, cnt_nonneg))

    def n_active(lo, hi, clo):
        return jnp.max(jnp.where((clo != k_row) & (hi - 1 > lo), 1, 0))

    def search_cond(carry):
        it, nact = carry[0], carry[1]
        return (nact > 0) & (it < SEARCH_VALUE_ROUNDS + 33)

    def search_round(it, lo, hi, clo, chi, bisect):
        active = (clo != k_row) & (hi - 1 > lo)
        lov = _key_to_float(lo)
        hiv = _key_to_float(hi)
        log_clo = jnp.log(clo.astype(F32) + 0.5)
        frac_interp = (log_clo - log_k) / (log_clo - jnp.log(chi.astype(F32) + 0.5))
        frac = jnp.where(bisect, 0.5, frac_interp)
        cand_value = _float_to_key(lov + (hiv - lov) * frac)
        cand_key = lo + ((hi - lo) >> 1)
        cand = jnp.where((it >= SEARCH_VALUE_ROUNDS) | zero_case, cand_key, cand_value)
        cand = jnp.minimum(jnp.maximum(cand, lo + 1), hi - 1)
        cand = jnp.where(active, cand, lo)
        cnt = count(lambda ky: ky >= cand)
        ok = cnt >= k_row
        return (jnp.where(ok, cand, lo), jnp.where(ok, hi, cand),
                jnp.where(ok, cnt, clo), jnp.where(ok, chi, cnt))

    log_k = jnp.log(k_row.astype(F32) + 0.5)

    def search_body(carry):
        it, _, lo, hi, clo, chi = carry
        group = it // SEARCH_ROUNDS_PER_CHECK
        for r in range(SEARCH_ROUNDS_PER_CHECK):
            if r == SEARCH_ROUNDS_PER_CHECK - 1:
                bisect = (group > 0) & (group % 2 == 0)
            else:
                bisect = group == 0
            lo, hi, clo, chi = search_round(it, lo, hi, clo, chi, bisect)
        return it + SEARCH_ROUNDS_PER_CHECK, n_active(lo, hi, clo), lo, hi, clo, chi

    _, _, thr, _, nge, _ = lax.while_loop(
        search_cond, search_body, (jnp.int32(0), n_active(lo0, hi0, clo0), lo0, hi0, clo0, chi0))

    cut_sc[...] = jnp.full((1, tq), n_kt * tq, jnp.int32)
    has_ties = jnp.max(nge - k_row) > 0

    @pl.when(has_ties)
    def _():
        n_gt = count(lambda ky: ky > thr)
        need = (k_row - n_gt).astype(F32)
        tri = jnp.where(lax.broadcasted_iota(jnp.int32, (tq, tq), 1) <= key_off, 1.0, 0.0).astype(BF16)

        def tie_tile(kt, carry):
            seen, best = carry
            eq = slab[kt] == thr
            rank = jnp.dot(tri, jnp.where(eq, 1.0, 0.0).astype(BF16),
                           preferred_element_type=F32) + seen
            hit = eq & (rank == need)
            best = jnp.maximum(best, fold(jnp.where(hit, kt * tq + key_off, -1), jnp.max))
            return rank[tq - 1:tq, :], best

        _, best = lax.fori_loop(0, n_kt, tie_tile,
                                (jnp.zeros((1, tq), F32), jnp.full(part, -1, jnp.int32)))
        cut_sc[...] = jnp.max(best, axis=0, keepdims=True)

    cut = cut_sc[...]

    m_sc[...] = jnp.full(m_sc.shape, NEG_BIG, F32)
    acc_sc[...] = jnp.zeros(acc_sc.shape, F32)

    n_steps = A_HEADS // FLASH_HEADS

    def mask_bias(kt):
        keys = slab[kt]
        sel = (keys > thr) | ((keys == thr) & ((kt * tq + key_off) <= cut))
        return jnp.where(sel, 0.0, NEG_BIG).astype(BF16)

    def logits(kt, st, bias):
        qs = q_ref[st * FLASH_HEADS:(st + 1) * FLASH_HEADS].reshape(FLASH_HEADS * tq, A_HEAD_DIM)
        kg = k_ref[st * FLASH_HEADS // A_REP, pl.ds(pl.multiple_of(kt * tq, tq), tq), :]
        s = lax.dot_general(kg, qs, _NT, preferred_element_type=F32).astype(BF16)
        return s + jnp.concatenate([bias] * FLASH_HEADS, axis=1)

    def attn_tile(kt, carry):
        s_next, bias = carry
        for st in range(n_steps):
            s = s_next
            if st + 1 < n_steps:
                s_next = logits(kt, st + 1, bias)
            else:
                kt_next = jnp.minimum(kt + 1, n_kt - 1)
                bias = mask_bias(kt_next)
                s_next = logits(kt_next, 0, bias)
            m_prev = m_sc[st]
            m_new = jnp.maximum(m_prev, jnp.max(s, axis=0, keepdims=True).astype(F32))
            alpha = jnp.exp2(m_prev - m_new)
            p = jnp.exp2(s - m_new.astype(BF16))
            acc_sc[st] = alpha * acc_sc[st] + jnp.dot(vt_ref[st * FLASH_HEADS // A_REP, kt], p,
                                                      preferred_element_type=F32)
            m_sc[st] = m_new
        return s_next, bias

    def attn_quad(j, carry):
        for u in range(4):
            carry = attn_tile(4 * j + u, carry)
        return carry

    bias0 = mask_bias(0)
    n_quads = n_kt // 4
    carry = lax.fori_loop(0, n_quads, attn_quad, (logits(0, 0, bias0), bias0))
    lax.fori_loop(4 * n_quads, n_kt, attn_tile, carry)

    for st in range(n_steps):
        acc = acc_sc[st]
        og = acc[:A_HEAD_DIM] / acc[A_HEAD_DIM:A_HEAD_DIM + 1]
        for r in range(FLASH_HEADS):
            hh = st * FLASH_HEADS + r
            ot_sc[hh * A_HEAD_DIM:(hh + 1) * A_HEAD_DIM, :] = og[:, r * tq:(r + 1) * tq]
    o_ref[...] = ot_sc[...].T.astype(BF16)


def _dsa_attn(q, k, vt, iq, ik, iwt, *, batch, seq, tq):
    t = batch * seq
    nq = seq // tq
    top_k = min(IDX_TOPK_MAX, seq // 4)
    qtile = lambda b, i: (0, b * nq + i, 0)
    whole = lambda b, i: (0, b, 0)
    once = pl.Buffered(1)
    return pl.pallas_call(
        functools.partial(_dsa_attn_kernel, tq=tq, n_qt=nq, top_k=top_k),
        grid=(batch, nq),
        in_specs=[pl.BlockSpec((A_HEADS, tq, A_HEAD_DIM), qtile),
                  pl.BlockSpec((IDX_HEADS, tq, IDX_DIM), qtile),
                  pl.BlockSpec((IDX_HEADS, tq), lambda b, i: (0, b * nq + i)),
                  pl.BlockSpec((A_KV_HEADS, seq, A_HEAD_DIM), whole, pipeline_mode=once),
                  pl.BlockSpec((A_KV_HEADS, nq, PV_ROWS, tq), lambda b, i: (0, b, 0, 0),
                               pipeline_mode=once),
                  pl.BlockSpec((seq, IDX_DIM), lambda b, i: (b, 0), pipeline_mode=once)],
        out_specs=pl.BlockSpec((tq, A_Q), lambda b, i: (b * nq + i, 0)),
        out_shape=jax.ShapeDtypeStruct((t, A_Q), BF16),
        scratch_shapes=[pltpu.VMEM((nq + nq % 2, tq, tq), jnp.int32),
                        pltpu.VMEM((A_HEADS // FLASH_HEADS, 1, FLASH_HEADS * tq), F32),
                        pltpu.VMEM((A_HEADS // FLASH_HEADS, PV_ROWS, FLASH_HEADS * tq), F32),
                        pltpu.VMEM((1, tq), jnp.int32),
                        pltpu.VMEM((A_Q, tq), F32)],
        compiler_params=_params("parallel", "arbitrary"),
        name="dsa_attn",
    )(q, iq, iwt, k, vt, ik)


def _mix_mlp_kernel(y_ref, wo_ref, h_ref, g1_ref, b1_ref, wu_ref, wd_ref, g2_ref, b2_ref, o_ref, *, tf):
    mix = jnp.dot(y_ref[...], wo_ref[...], preferred_element_type=F32)
    h = _layer_norm(DN_ALPHA * h_ref[...] + mix, g1_ref[...], b1_ref[...])
    xb = h.astype(BF16)
    acc = jnp.zeros(h.shape, F32)
    for c in range(D_FF // tf):
        u = jnp.dot(xb, wu_ref[:, c * tf:(c + 1) * tf], preferred_element_type=F32)
        u = jnp.square(jnp.maximum(u, 0.0)).astype(BF16)
        acc = acc + jnp.dot(u, wd_ref[c * tf:(c + 1) * tf, :], preferred_element_type=F32)
    o_ref[...] = _layer_norm(DN_ALPHA * h + acc, g2_ref[...], b2_ref[...])


def _mix_mlp(y, wo, h, g1, b1, wu, wd, g2, b2, *, tm, tf):
    t, kdim = y.shape
    row = lambda i: (i, 0)
    once = pl.Buffered(1)
    resident = lambda w: pl.BlockSpec(w.shape, lambda i: (0, 0), pipeline_mode=once)
    return pl.pallas_call(
        functools.partial(_mix_mlp_kernel, tf=tf),
        grid=(t // tm,),
        in_specs=[pl.BlockSpec((tm, kdim), row), resident(wo), pl.BlockSpec((tm, D_MODEL), row),
                  _const_spec(g1.shape), _const_spec(b1.shape), resident(wu), resident(wd),
                  _const_spec(g2.shape), _const_spec(b2.shape)],
        out_specs=pl.BlockSpec((tm, D_MODEL), row),
        out_shape=jax.ShapeDtypeStruct((t, D_MODEL), F32),
        compiler_params=_params("parallel"),
        name="mix_mlp",
    )(y, wo, h, g1, b1, wu, wd, g2, b2)


def _gla_proj_kernel(h_ref, wq_ref, wk_ref, wv_ref, wr_ref, wa_ref,
                     q_ref, k_ref, v_ref, r_ref, a_ref):
    xb = h_ref[...].astype(BF16)
    q = jnp.dot(xb, wq_ref[...], preferred_element_type=F32)
    q_ref[...] = (q * (B_DK ** -0.5)).astype(BF16)
    k_ref[...] = jnp.dot(xb, wk_ref[...], preferred_element_type=F32).astype(BF16)
    v_ref[...] = jnp.dot(xb, wv_ref[...], preferred_element_type=F32).astype(BF16)
    r_ref[...] = jnp.dot(xb, wr_ref[...], preferred_element_type=F32).astype(BF16)
    a = jnp.dot(xb, wa_ref[...], preferred_element_type=F32)
    a_ref[...] = a[:, :B_GATE_RANK]


def _gla_proj(h, wq, wk, wv, wr, wa, *, tm):
    t = h.shape[0]
    row = lambda i: (i, 0)
    return pl.pallas_call(
        _gla_proj_kernel,
        grid=(t // tm,),
        in_specs=[pl.BlockSpec((tm, D_MODEL), row), _const_spec(wq.shape), _const_spec(wk.shape),
                  _const_spec(wv.shape), _const_spec(wr.shape), _const_spec(wa.shape)],
        out_specs=[pl.BlockSpec((tm, B_QK), row), pl.BlockSpec((tm, B_QK), row),
                   pl.BlockSpec((tm, B_V), row), pl.BlockSpec((tm, B_V), row),
                   pl.BlockSpec((tm, B_GATE_RANK), row)],
        out_shape=[jax.ShapeDtypeStruct((t, B_QK), BF16), jax.ShapeDtypeStruct((t, B_QK), BF16),
                   jax.ShapeDtypeStruct((t, B_V), BF16), jax.ShapeDtypeStruct((t, B_V), BF16),
                   jax.ShapeDtypeStruct((t, B_GATE_RANK), F32)],
        compiler_params=_params("parallel"),
        name="gla_proj",
    )(h, wq, wk, wv, wr, wa)


def _gla_kernel(q_ref, k_ref, v_ref, r_ref, a_ref, w2_ref, ba_ref, gn_ref, y_ref, st_ref, *, rows, heads):
    n_chunks = rows // B_CHUNK

    @pl.when(pl.program_id(2) == 0)
    def _():
        st_ref[...] = jnp.zeros(st_ref.shape, F32)

    x = jnp.dot(a_ref[...], w2_ref[...], preferred_element_type=F32,
                precision=lax.Precision.HIGHEST) + ba_ref[...]
    log_a = (jnp.minimum(x, 0.0) - jnp.log1p(jnp.exp(-jnp.abs(x)))) / B_GATE_TAU

    row_in_chunk = lax.broadcasted_iota(jnp.int32, log_a.shape, 0) % B_CHUNK
    b = log_a
    step = 1
    while step < B_CHUNK:
        b = b + jnp.where(row_in_chunk >= step, pltpu.roll(b, step, axis=0), 0.0)
        step *= 2

    tril = (lax.broadcasted_iota(jnp.int32, (B_CHUNK, B_CHUNK), 1)
            <= lax.broadcasted_iota(jnp.int32, (B_CHUNK, B_CHUNK), 0))
    gn = gn_ref[...]

    prep = []
    for hd in range(heads):
        ks = slice(hd * B_DK, (hd + 1) * B_DK)
        b3 = b[:, ks].reshape(n_chunks, B_CHUNK, B_DK)
        b_last = b3[:, B_CHUNK - 1:B_CHUNK, :]
        b_mid = b3[:, B_CHUNK // 2 - 1:B_CHUNK // 2, :]
        q3 = q_ref[:, ks].astype(F32).reshape(n_chunks, B_CHUNK, B_DK)
        k3 = k_ref[:, ks].astype(F32).reshape(n_chunks, B_CHUNK, B_DK)
        prep.append(dict(
            q_in=(q3 * jnp.exp(b3)).astype(BF16),
            k_out=(k3 * jnp.exp(b_last - b3)).astype(BF16),
            q_mid=(q3 * jnp.exp(jnp.minimum(b3 - b_mid, EXP_CLAMP))).astype(BF16),
            k_mid=(k3 * jnp.exp(jnp.minimum(b_mid - b3, EXP_CLAMP))).astype(BF16),
            decay=jnp.exp(b_last),
            state=st_ref[hd]))

    for n in range(n_chunks):
        lo, hi = n * B_CHUNK, (n + 1) * B_CHUNK
        for hd in range(heads):
            p = prep[hd]
            vs = slice(hd * B_DV, (hd + 1) * B_DV)
            vn = v_ref[lo:hi, vs]
            attn = lax.dot_general(p["q_mid"][n], p["k_mid"][n], _NT, preferred_element_type=F32)
            attn = jnp.where(tril, attn, 0.0).astype(BF16)
            o = jnp.dot(attn, vn, preferred_element_type=F32)
            o = o + lax.dot_general(p["q_in"][n], p["state"].astype(BF16), _NT,
                                    preferred_element_type=F32)
            upd = lax.dot_general(vn, p["k_out"][n], _TN, preferred_element_type=F32)
            p["state"] = p["decay"][n] * p["state"] + upd
            o = o * lax.rsqrt(jnp.mean(o * o, axis=-1, keepdims=True) + RMS_EPS) * gn
            rn = r_ref[lo:hi, vs].astype(F32)
            gate = rn / (1.0 + jnp.exp(-rn))
            y_ref[lo:hi, vs] = (o * gate).astype(BF16)
    for hd in range(heads):
        st_ref[hd] = prep[hd]["state"]


def _gla(q, k, v, r, a, w2, ba, gn, *, batch, seq, rows, heads):
    t = batch * seq
    nb = seq // rows
    tile = lambda b, h, j: (b * nb + j, h)
    return pl.pallas_call(
        functools.partial(_gla_kernel, rows=rows, heads=heads),
        grid=(batch, B_HEADS // heads, nb),
        in_specs=[pl.BlockSpec((rows, heads * B_DK), tile), pl.BlockSpec((rows, heads * B_DK), tile),
                  pl.BlockSpec((rows, heads * B_DV), tile), pl.BlockSpec((rows, heads * B_DV), tile),
                  pl.BlockSpec((rows, B_GATE_RANK), lambda b, h, j: (b * nb + j, 0)),
                  pl.BlockSpec((B_GATE_RANK, heads * B_DK), lambda b, h, j: (0, h)),
                  pl.BlockSpec((1, heads * B_DK), lambda b, h, j: (0, h)),
                  _const_spec(gn.shape)],
        out_specs=pl.BlockSpec((rows, heads * B_DV), tile),
        out_shape=jax.ShapeDtypeStruct((t, B_V), BF16),
        scratch_shapes=[pltpu.VMEM((heads, B_DV, B_DK), F32)],
        compiler_params=_params("parallel", "parallel", "arbitrary"),
        name="gla",
    )(q, k, v, r, a, w2, ba, gn)


def _tile(n, pref):
    return pref if n % pref == 0 else n


def _forward(x, positions, a_w_in, a_w_o, b_w_in, b_w_a2, b_b_a, b_g_norm, b_w_o,
             ln_mix_g, ln_mix_b, mlp_w_up, mlp_w_down, ln_mlp_g, ln_mlp_b):
    batch, seq, _ = x.shape
    t = batch * seq
    tm = _tile(t, 512)
    tq = _tile(seq, 256)
    rows = _tile(seq, 512)

    h = x.reshape(t, D_MODEL)
    pos = positions.reshape(t, 1).astype(F32)
    inv = ROPE_THETA ** (-jnp.arange(0, ROT_DIM, 2, dtype=F32) / ROT_DIM)
    inv = jnp.concatenate([inv, inv, jnp.zeros((A_HEAD_DIM - ROT_DIM,), F32)])
    inv = jnp.tile(inv, LANES // A_HEAD_DIM).reshape(1, LANES)
    row = lambda p: p.reshape(1, -1)

    for layer in range(DEPTH):
        j = layer // 2
        if layer % 2 == 0:
            w = a_w_in[j].astype(BF16)
            o1, o2, o3, o4 = A_Q, A_Q + A_KV, A_Q + 2 * A_KV, A_Q + 2 * A_KV + A_IQ
            o5 = o4 + IDX_DIM
            wik = jnp.pad(w[:, o4:o5], ((0, 0), (0, LANES - IDX_DIM)))
            wiwt = jnp.pad(w[:, o5:].T, ((0, BF16_SUBLANES - IDX_HEADS), (0, 0)))
            q, k, vt, iq, ik, iwt = _dsa_proj(h, pos, inv, w[:, :o1], w[:, o1:o2], w[:, o2:o3].T,
                                              w[:, o3:o4], wik, wiwt, tm=tm, tkv=tq)
            y = _dsa_attn(q, k, vt, iq, ik, iwt, batch=batch, seq=seq, tq=tq)
            w_o = a_w_o[j].astype(BF16)
        else:
            w = b_w_in[j].astype(BF16)
            o1, o2, o3, o4 = B_QK, 2 * B_QK, 2 * B_QK + B_V, 2 * B_QK + 2 * B_V
            wa = jnp.pad(w[:, o4:], ((0, 0), (0, LANES - B_GATE_RANK)))
            q, k, v, r, a = _gla_proj(h, w[:, :o1], w[:, o1:o2], w[:, o2:o3], w[:, o3:o4], wa, tm=tm)
            y = _gla(q, k, v, r, a, b_w_a2[j], row(b_b_a[j]), row(b_g_norm[j]),
                     batch=batch, seq=seq, rows=rows, heads=GLA_HEADS_PER_STEP)
            w_o = b_w_o[j].astype(BF16)
        h = _mix_mlp(y, w_o, h, row(ln_mix_g[layer]), row(ln_mix_b[layer]),
                     mlp_w_up[layer].astype(BF16), mlp_w_down[layer].astype(BF16),
                     row(ln_mlp_g[layer]), row(ln_mlp_b[layer]), tm=tm, tf=1024)
    return h.reshape(batch, seq, D_MODEL)


_forward_jit = jax.jit(_forward)


def kernel(x, positions, a_w_in, a_w_o, b_w_in, b_w_a2, b_b_a, b_g_norm, b_w_o,
           ln_mix_g, ln_mix_b, mlp_w_up, mlp_w_down, ln_mlp_g, ln_mlp_b):
    return _forward_jit(x, positions, a_w_in, a_w_o, b_w_in, b_w_a2, b_b_a, b_g_norm, b_w_o,
                        ln_mix_g, ln_mix_b, mlp_w_up, mlp_w_down, ln_mlp_g, ln_mlp_b)
```
